```python
import jax, jax.numpy as jnp
from jax import lax
import numpy as np

D_MODEL = 4096
BATCH = 1
SEQ = 16384
DEPTH = 4

MEM_LEN = 256
CHUNK = 64
EPS = 1e-6
MIX_V = D_MODEL // 4
GLA_HEADS = 4
GLA_DV = MIX_V // GLA_HEADS
GLA_DK = GLA_DV // 2
GLA_QK = GLA_HEADS * GLA_DK
GLA_V = GLA_HEADS * GLA_DV
GLA_RANK = 16
GLA_TAU = 16.0
ML_HEADS = 4
ML_DV = MIX_V // ML_HEADS
ML_DK = ML_DV // 2
ML_QK = ML_HEADS * ML_DK
ML_V = ML_HEADS * ML_DV
CONV_W = 4
X_HEADS = 4
X_DH = MIX_V // X_HEADS
X_W = X_HEADS * X_DH
N_BRANCH = 3
GATE_RANK = 256
D_FF = 4 * D_MODEL

IN_SPLITS = (GLA_QK, GLA_QK, GLA_V, GLA_V, GLA_RANK,
             2 * ML_QK, ML_V, ML_V, ML_HEADS, ML_HEADS,
             X_W, GATE_RANK)
IN_WIDTH = sum(IN_SPLITS)

kernel_name = "hybrid_gla_mlstm_memory_trunk"


def split_columns(z):
    idx, acc = [], 0
    for w in IN_SPLITS[:-1]:
        acc += w
        idx.append(acc)
    return jnp.split(z, idx, axis=-1)


def rmsnorm(x, g):
    xf = x.astype(jnp.float32)
    xf = xf * lax.rsqrt(jnp.mean(xf * xf, axis=-1, keepdims=True) + EPS)
    return xf.astype(x.dtype) * g


def head_rmsnorm(o, heads, g):
    B, S, W = o.shape
    of = o.astype(jnp.float32).reshape(B, S, heads, W // heads)
    of = of * lax.rsqrt(jnp.mean(of * of, axis=-1, keepdims=True) + EPS)
    return of.reshape(B, S, W).astype(o.dtype) * g


def to_chunks(t, heads):
    B, S, W = t.shape
    return t.reshape(B, S // CHUNK, CHUNK, heads, W // heads).transpose(0, 3, 1, 2, 4)


def gate_chunks(t):
    B, S, H = t.shape
    return t.reshape(B, S // CHUNK, CHUNK, H).transpose(0, 3, 1, 2)


def from_chunks(t):
    B, H, N, C, d = t.shape
    return t.transpose(0, 2, 3, 1, 4).reshape(B, N * C, H * d)


def causal_dwconv(u, w, b):
    y = lax.conv_general_dilated(
        u, w[:, None, :].astype(u.dtype), window_strides=(1,),
        padding=[(CONV_W - 1, 0)], dimension_numbers=("NWC", "WIO", "NWC"),
        feature_group_count=u.shape[-1])
    return y + b.astype(u.dtype)


def gla_mixer(q, k, v, log_a):
    q = to_chunks(q, GLA_HEADS) * (GLA_DK ** -0.5)
    k = to_chunks(k, GLA_HEADS)
    v = to_chunks(v, GLA_HEADS)
    b = jnp.cumsum(to_chunks(log_a, GLA_HEADS), axis=3)
    causal = jnp.tril(jnp.ones((CHUNK, CHUNK), dtype=bool))
    q_d = q * jnp.exp(b)
    att = jnp.einsum("bhnck,bhnsk->bhncs", q_d, k * jnp.exp(-b))
    att = jnp.where(causal, att, 0.0)
    o_intra = jnp.einsum("bhncs,bhnsv->bhncv", att, v)
    b_last = b[:, :, :, -1:, :]
    d_state = jnp.einsum("bhnck,bhncv->bhnkv", k * jnp.exp(b_last - b), v)
    decay = jnp.exp(b_last[:, :, :, 0, :])

    def step(S, inp):
        dec, ds = inp
        return dec[..., None] * S + ds, S

    B, H = q.shape[0], q.shape[1]
    S0 = jnp.zeros((B, H, GLA_DK, GLA_DV), jnp.float32)
    _, S_prev = lax.scan(step, S0, (jnp.moveaxis(decay, 2, 0), jnp.moveaxis(d_state, 2, 0)))
    S_prev = jnp.moveaxis(S_prev, 0, 2)
    o_inter = jnp.einsum("bhnck,bhnkv->bhncv", q_d, S_prev)
    return from_chunks(o_intra + o_inter)


def mlstm_mixer(q, k, v, i_pre, f_pre):
    q = to_chunks(q, ML_HEADS)
    k = to_chunks(k, ML_HEADS) * (ML_DK ** -0.5)
    v = to_chunks(v, ML_HEADS)
    log_i = gate_chunks(i_pre)
    F = jnp.cumsum(gate_chunks(jax.nn.log_sigmoid(f_pre)), axis=3)
    causal = jnp.tril(jnp.ones((CHUNK, CHUNK), dtype=bool))
    L = jnp.where(causal, F[..., :, None] - F[..., None, :] + log_i[..., None, :], -jnp.inf)
    F_last = F[..., -1]
    G = F_last[..., None] - F + log_i
    g = jnp.max(G, axis=-1)
    wG = jnp.exp(G - g[..., None])
    dC = jnp.einsum("bhnc,bhnck,bhncv->bhnkv", wG, k, v)
    dn = jnp.einsum("bhnc,bhnck->bhnk", wG, k)

    def step(carry, inp):
        C, n, m = carry
        fl, gl, dc, dnn = inp
        m_new = jnp.maximum(fl + m, gl)
        a = jnp.exp(fl + m - m_new)
        bb = jnp.exp(gl - m_new)
        C_new = a[..., None, None] * C + bb[..., None, None] * dc
        n_new = a[..., None] * n + bb[..., None] * dnn
        return (C_new, n_new, m_new), (C, n, m)

    B, H = q.shape[0], q.shape[1]
    init = (jnp.zeros((B, H, ML_DK, ML_DV), jnp.float32),
            jnp.zeros((B, H, ML_DK), jnp.float32),
            jnp.full((B, H), -1e30, jnp.float32))
    xs = (jnp.moveaxis(F_last, 2, 0), jnp.moveaxis(g, 2, 0),
          jnp.moveaxis(dC, 2, 0), jnp.moveaxis(dn, 2, 0))
    _, (C_prev, n_prev, m_prev) = lax.scan(step, init, xs)
    C_prev = jnp.moveaxis(C_prev, 0, 2)
    n_prev = jnp.moveaxis(n_prev, 0, 2)
    m_prev = jnp.moveaxis(m_prev, 0, 2)

    m_inter = F + m_prev[..., None]
    m = jnp.maximum(jnp.max(L, axis=-1), m_inter)
    s = jnp.einsum("bhnck,bhnsk->bhncs", q, k) * jnp.exp(L - m[..., None])
    w_inter = jnp.exp(m_inter - m)
    num = jnp.einsum("bhncs,bhnsv->bhncv", s, v) + \
        w_inter[..., None] * jnp.einsum("bhnck,bhnkv->bhncv", q, C_prev)
    den = jnp.sum(s, axis=-1) + w_inter * jnp.einsum("bhnck,bhnk->bhnc", q, n_prev)
    h = num / jnp.maximum(jnp.abs(den), jnp.exp(-m))[..., None]
    return from_chunks(h)


def memory_attention(q, km, vm):
    B, S, _ = q.shape
    M = km.shape[1]
    qh = q.reshape(B, S, X_HEADS, X_DH)
    kh = km.reshape(B, M, X_HEADS, X_DH)
    vh = vm.reshape(B, M, X_HEADS, X_DH)
    s = jnp.einsum("bshd,bmhd->bhsm", qh, kh).astype(jnp.float32) * (X_DH ** -0.5)
    p = jax.nn.softmax(s, axis=-1).astype(vh.dtype)
    return jnp.einsum("bhsm,bmhd->bshd", p, vh).reshape(B, S, X_W)


def setup_inputs(seed: int = 0) -> dict:
    key = jax.random.key(seed)
    ks = jax.random.split(key, 24)
    f32 = jnp.float32

    def nrm(k, shape, scale):
        return jax.random.normal(k, shape, f32) * scale

    res_scale = (2 * DEPTH) ** -0.5
    b_f = jnp.broadcast_to(jnp.linspace(3.0, 6.0, ML_HEADS, dtype=f32), (DEPTH, ML_HEADS))
    return {
        "x": nrm(ks[0], (BATCH, SEQ, D_MODEL), 1.0),
        "mem": nrm(ks[1], (BATCH, MEM_LEN, D_MODEL), 1.0),
        "norm_mix": 1.0 + nrm(ks[2], (DEPTH, D_MODEL), 0.01),
        "w_in": nrm(ks[3], (DEPTH, D_MODEL, IN_WIDTH), D_MODEL ** -0.5),
        "w_gla_a": nrm(ks[4], (DEPTH, GLA_RANK, GLA_QK), GLA_RANK ** -0.5),
        "b_gla_a": nrm(ks[5], (DEPTH, GLA_QK), 0.1),
        "gla_norm": 1.0 + nrm(ks[6], (DEPTH, GLA_V), 0.01),
        "conv_w": nrm(ks[7], (DEPTH, CONV_W, 2 * ML_QK), CONV_W ** -0.5),
        "conv_b": nrm(ks[8], (DEPTH, 2 * ML_QK), 0.01),
        "b_ml_i": nrm(ks[9], (DEPTH, ML_HEADS), 0.1),
        "b_ml_f": b_f + nrm(ks[10], (DEPTH, ML_HEADS), 0.1),
        "ml_norm": 1.0 + nrm(ks[11], (DEPTH, ML_V), 0.01),
        "mem_norm": 1.0 + nrm(ks[12], (D_MODEL,), 0.01),
        "w_mem_k": nrm(ks[13], (DEPTH, D_MODEL, X_W), D_MODEL ** -0.5),
        "w_mem_v": nrm(ks[14], (DEPTH, D_MODEL, X_W), D_MODEL ** -0.5),
        "w_branch": nrm(ks[15], (DEPTH, N_BRANCH, MIX_V, D_MODEL), MIX_V ** -0.5),
        "w_gate": nrm(ks[16], (DEPTH, N_BRANCH, GATE_RANK, D_MODEL), GATE_RANK ** -0.5),
        "b_gate": nrm(ks[17], (DEPTH, N_BRANCH, D_MODEL), 0.01),
        "w_out": nrm(ks[18], (DEPTH, D_MODEL, D_MODEL), D_MODEL ** -0.5 * res_scale),
        "norm_ffn": 1.0 + nrm(ks[19], (DEPTH, D_MODEL), 0.01),
        "w_ff1": nrm(ks[20], (DEPTH, D_MODEL, D_FF), D_MODEL ** -0.5),
        "w_ff2": nrm(ks[21], (DEPTH, D_FF, D_MODEL), D_FF ** -0.5 * res_scale),
        "final_norm": 1.0 + nrm(ks[22], (D_MODEL,), 0.01),
    }


def reference(x, mem, norm_mix, w_in, w_gla_a, b_gla_a, gla_norm, conv_w, conv_b,
              b_ml_i, b_ml_f, ml_norm, mem_norm, w_mem_k, w_mem_v, w_branch,
              w_gate, b_gate, w_out, norm_ffn, w_ff1, w_ff2, final_norm):
    f32 = jnp.float32
    memn = rmsnorm(mem, mem_norm)
    for l in range(DEPTH):
        h = rmsnorm(x, norm_mix[l])
        z = h @ w_in[l]
        (gq, gk, gv, gg, ga, mqk, mv, mo, mi, mf, xq, gate_lat) = split_columns(z)

        log_a = jax.nn.log_sigmoid((ga @ w_gla_a[l] + b_gla_a[l]).astype(f32)) / GLA_TAU
        o_gla = gla_mixer(gq.astype(f32), gk.astype(f32), gv.astype(f32), log_a).astype(x.dtype)
        o_gla = head_rmsnorm(o_gla, GLA_HEADS, gla_norm[l]) * jax.nn.silu(gg)

        mqk = jax.nn.silu(causal_dwconv(mqk, conv_w[l], conv_b[l]))
        mq, mk = jnp.split(mqk, 2, axis=-1)
        h_ml = mlstm_mixer(mq.astype(f32), mk.astype(f32), mv.astype(f32),
                           (mi + b_ml_i[l]).astype(f32), (mf + b_ml_f[l]).astype(f32))
        o_ml = jax.nn.sigmoid(mo) * h_ml.astype(x.dtype)
        o_ml = head_rmsnorm(o_ml, ML_HEADS, ml_norm[l])

        o_mem = memory_attention(xq, memn @ w_mem_k[l], memn @ w_mem_v[l])

        y = None
        for j, o_b in enumerate((o_gla, o_ml, o_mem)):
            gate = jax.nn.sigmoid(gate_lat @ w_gate[l, j] + b_gate[l, j])
            term = gate * (o_b @ w_branch[l, j])
            y = term if j == 0 else y + term
        x = x + y @ w_out[l]

        h = rmsnorm(x, norm_ffn[l])
        x = x + jnp.square(jax.nn.relu(h @ w_ff1[l])) @ w_ff2[l]
    return rmsnorm(x, final_norm)
```

```python
import functools

import jax
import jax.numpy as jnp
from jax import lax
from jax.experimental import pallas as pl
from jax.experimental.pallas import tpu as pltpu

F32 = jnp.float32
BF16 = jnp.bfloat16

EPS = 1e-6
CHUNK = 64
N_HEADS = 4
QK_DIM = 128
V_DIM = 256
GLA_RANK = 16
GLA_TAU = 16.0
CONV_W = 4
MEM_LEN = 256
GATE_RANK = 256
N_BRANCH = 3

LANE = 128
SUBLANE = 8
VMEM_LIMIT = 56 * 1024 * 1024

COL_GV, COL_GG, COL_MV, COL_MO, COL_XQ = 0, 8, 16, 24, 32
COL_MQ, COL_MK, COL_GQ, COL_GK = 40, 44, 48, 52
COL_GATE, COL_SMALL, IN_COLS = 56, 58, 60
SM_GA, SM_MI, SM_MF = 0, 16, 20


def _cparams(sem):
    return pltpu.CompilerParams(dimension_semantics=sem, vmem_limit_bytes=VMEM_LIMIT)


def _rmsnorm_kernel(x_ref, g_ref, o_ref):
    x = x_ref[...]
    ms = jnp.mean(x * x, axis=-1, keepdims=True)
    o_ref[...] = ((x * lax.rsqrt(ms + EPS)) * g_ref[...]).astype(o_ref.dtype)


def rmsnorm(x, g, out_dtype, tm=256):
    m, d = x.shape
    return pl.pallas_call(
        _rmsnorm_kernel,
        grid=(m // tm,),
        in_specs=[pl.BlockSpec((tm, d), lambda i: (i, 0)),
                  pl.BlockSpec((1, d), lambda i: (0, 0))],
        out_specs=pl.BlockSpec((tm, d), lambda i: (i, 0)),
        out_shape=jax.ShapeDtypeStruct((m, d), out_dtype),
        compiler_params=_cparams(("parallel",)),
        name="rmsnorm",
    )(x, g.reshape(1, d))


def _mm_kernel(*refs, nk, epilogue, has_res):
    if has_res:
        a_ref, b_ref, r_ref, o_ref = refs[:4]
        scratch = refs[4:]
    else:
        a_ref, b_ref, o_ref = refs[:3]
        r_ref = None
        scratch = refs[3:]

    def finish(acc):
        if epilogue == "relu2":
            acc = jnp.square(jnp.maximum(acc, 0.0))
        if r_ref is not None:
            acc = r_ref[...] + acc
        o_ref[...] = acc.astype(o_ref.dtype)

    part = jnp.dot(a_ref[...].astype(BF16), b_ref[...].astype(BF16),
                   preferred_element_type=F32)
    if nk == 1:
        finish(part)
    else:
        acc_ref, = scratch
        k = pl.program_id(2)

        @pl.when(k == 0)
        def _():
            acc_ref[...] = part

        @pl.when(jnp.logical_and(k > 0, k < nk - 1))
        def _():
            acc_ref[...] += part

        @pl.when(k == nk - 1)
        def _():
            finish(acc_ref[...] + part)


def matmul(a, b, *, tm, tn, tk=None, out_dtype, epilogue=None, residual=None, name):
    m, kdim = a.shape
    _, n = b.shape
    tk = kdim if tk is None else tk
    nk = kdim // tk
    in_specs = [pl.BlockSpec((tm, tk), lambda i, j, k: (i, k)),
                pl.BlockSpec((tk, tn), lambda i, j, k: (k, j))]
    args = [a, b]
    if residual is not None:
        in_specs.append(pl.BlockSpec((tm, tn), lambda i, j, k: (i, j)))
        args.append(residual)
    scratch = [pltpu.VMEM((tm, tn), F32)] if nk > 1 else []
    return pl.pallas_call(
        functools.partial(_mm_kernel, nk=nk, epilogue=epilogue,
                          has_res=residual is not None),
        grid=(m // tm, n // tn, nk),
        in_specs=in_specs,
        out_specs=pl.BlockSpec((tm, tn), lambda i, j, k: (i, j)),
        out_shape=jax.ShapeDtypeStruct((m, n), out_dtype),
        scratch_shapes=scratch,
        compiler_params=_cparams(("parallel", "parallel", "arbitrary")),
        name=name,
    )(*args)


def _log_sigmoid(x):
    return jnp.minimum(x, 0.0) - jnp.log1p(jnp.exp(-jnp.abs(x)))


def _sigmoid(x):
    return 1.0 / (1.0 + jnp.exp(-x))


def _chunk_tri(t):
    r = lax.broadcasted_iota(jnp.int32, (t, t), 0)
    c = lax.broadcasted_iota(jnp.int32, (t, t), 1)
    shift = CHUNK.bit_length() - 1
    same = jnp.right_shift(r, shift) == jnp.right_shift(c, shift)
    return jnp.where(jnp.logical_and(same, c <= r), 1.0, 0.0).astype(BF16)


def _chunk_cumsum(x, tri):
    hi = x.astype(BF16)
    r1 = x - hi.astype(F32)
    mid = r1.astype(BF16)
    lo = (r1 - mid.astype(F32)).astype(BF16)
    out = jnp.dot(tri, hi, preferred_element_type=F32)
    out += jnp.dot(tri, mid, preferred_element_type=F32)
    out += jnp.dot(tri, lo, preferred_element_type=F32)
    return out


def _dot_nt(a, b):
    return lax.dot_general(a, b, (((1,), (1,)), ((), ())), preferred_element_type=F32)


def _dot_tn(a, b):
    return lax.dot_general(a, b, (((0,), (0,)), ((), ())), preferred_element_type=F32)


def _causal_mask():
    r = lax.broadcasted_iota(jnp.int32, (CHUNK, CHUNK), 0)
    c = lax.broadcasted_iota(jnp.int32, (CHUNK, CHUNK), 1)
    return c <= r


def _head_rmsnorm(o, g):
    ms = jnp.mean(o * o, axis=-1, keepdims=True)
    return (o * lax.rsqrt(ms + EPS)) * g


def _gla_kernel(q_ref, k_ref, v_ref, gg_ref, sm_ref, wa_ref, ba_ref, gn_ref,
                o_ref, st_ref, *, t_blk):
    @pl.when(pl.program_id(1) == 0)
    def _():
        st_ref[...] = jnp.zeros_like(st_ref)

    pre = jnp.dot(sm_ref[...].astype(BF16), wa_ref[...],
                  preferred_element_type=F32) + ba_ref[...]
    log_a = _log_sigmoid(pre) / GLA_TAU
    b_all = _chunk_cumsum(log_a, _chunk_tri(t_blk))
    causal = _causal_mask()
    scale = QK_DIM ** -0.5
    for c in range(t_blk // CHUNK):
        rows = slice(c * CHUNK, (c + 1) * CHUNK)
        b = b_all[rows]
        b_last = b[CHUNK - 1:CHUNK]
        q = q_ref[rows, :] * scale
        k = k_ref[rows, :]
        v = v_ref[rows, :].astype(BF16)
        q_d = (q * jnp.exp(b)).astype(BF16)
        k_d = (k * jnp.exp(-b)).astype(BF16)
        k_s = (k * jnp.exp(b_last - b)).astype(BF16)
        att = jnp.where(causal, _dot_nt(q_d, k_d), 0.0)
        st = st_ref[...]
        o = jnp.dot(att.astype(BF16), v, preferred_element_type=F32)
        o += _dot_nt(q_d, st.astype(BF16))
        st_ref[...] = st * jnp.exp(b_last) + _dot_tn(v, k_s)
        gg = gg_ref[rows, :]
        o = _head_rmsnorm(o, gn_ref[...]) * (gg * _sigmoid(gg))
        o_ref[rows, :] = o.astype(o_ref.dtype)


def gla_branch(z, wa_pad, ba, gn, *, t_blk=256):
    s = z.shape[0]
    return pl.pallas_call(
        functools.partial(_gla_kernel, t_blk=t_blk),
        grid=(N_HEADS, s // t_blk),
        in_specs=[
            pl.BlockSpec((t_blk, QK_DIM), lambda h, t: (t, COL_GQ + h)),
            pl.BlockSpec((t_blk, QK_DIM), lambda h, t: (t, COL_GK + h)),
            pl.BlockSpec((t_blk, V_DIM), lambda h, t: (t, COL_GV // 2 + h)),
            pl.BlockSpec((t_blk, V_DIM), lambda h, t: (t, COL_GG // 2 + h)),
            pl.BlockSpec((t_blk, LANE), lambda h, t: (t, COL_SMALL)),
            pl.BlockSpec((LANE, QK_DIM), lambda h, t: (0, h)),
            pl.BlockSpec((1, QK_DIM), lambda h, t: (0, h)),
            pl.BlockSpec((1, V_DIM), lambda h, t: (0, h)),
        ],
        out_specs=pl.BlockSpec((t_blk, V_DIM), lambda h, t: (t, h)),
        out_shape=jax.ShapeDtypeStruct((s, N_HEADS * V_DIM), BF16),
        scratch_shapes=[pltpu.VMEM((V_DIM, QK_DIM), F32)],
        compiler_params=_cparams(("parallel", "arbitrary")),
        name="gla",
    )(z, z, z, z, z, wa_pad, ba, gn)


def _mlstm_kernel(q_ref, k_ref, v_ref, mo_ref, sm_ref, cwq_ref, cwk_ref, cbq_ref,
                  cbk_ref, gb_ref, mn_ref, o_ref,
                  uq_ref, uk_ref, c_ref, n_ref, m_ref, *, t_blk):
    h = pl.program_id(0)

    @pl.when(pl.program_id(1) == 0)
    def _():
        uq_ref[0:SUBLANE, :] = jnp.zeros((SUBLANE, QK_DIM), F32)
        uk_ref[0:SUBLANE, :] = jnp.zeros((SUBLANE, QK_DIM), F32)
        c_ref[...] = jnp.zeros_like(c_ref)
        n_ref[...] = jnp.zeros_like(n_ref)
        m_ref[...] = jnp.full(m_ref.shape, -1e30, F32)

    def conv_silu(u_ref, in_ref, w_ref, b_ref):
        u_ref[SUBLANE:SUBLANE + t_blk, :] = in_ref[...]
        y = b_ref[...]
        for j in range(CONV_W):
            off = SUBLANE - (CONV_W - 1) + j
            y = y + w_ref[j:j + 1, :] * u_ref[off:off + t_blk, :]
        u_ref[0:SUBLANE, :] = u_ref[t_blk:t_blk + SUBLANE, :]
        return y * _sigmoid(y)

    q_all = conv_silu(uq_ref, q_ref, cwq_ref, cbq_ref)
    k_all = conv_silu(uk_ref, k_ref, cwk_ref, cbk_ref) * (QK_DIM ** -0.5)

    pre = sm_ref[...] + gb_ref[...]
    f_cum = _chunk_cumsum(_log_sigmoid(pre), _chunk_tri(t_blk))
    lane = lax.broadcasted_iota(jnp.int32, (t_blk, LANE), 1)
    is_f = jnp.logical_and(lane >= SM_MF, lane < SM_MF + N_HEADS)
    cols = jnp.where(is_f, f_cum, pre)
    rows_t = cols.T
    sel_i = (lane == SM_MI + h).astype(F32)
    sel_f = (lane == SM_MF + h).astype(F32)
    li_col = jnp.sum(cols * sel_i, axis=-1, keepdims=True)
    f_col = jnp.sum(cols * sel_f, axis=-1, keepdims=True)
    sub = lax.broadcasted_iota(jnp.int32, (LANE, t_blk), 0)
    li_row = jnp.sum(jnp.where(sub == SM_MI + h, rows_t, 0.0), axis=0, keepdims=True)
    f_row = jnp.sum(jnp.where(sub == SM_MF + h, rows_t, 0.0), axis=0, keepdims=True)

    causal = _causal_mask()
    for c in range(t_blk // CHUNK):
        rows = slice(c * CHUNK, (c + 1) * CHUNK)
        q = q_all[rows]
        k = k_all[rows]
        v = v_ref[rows, :].astype(BF16)
        fc, lic = f_col[rows], li_col[rows]
        fr, lir = f_row[:, rows], li_row[:, rows]
        f_last = fr[:, CHUNK - 1:CHUNK]
        m_prev = m_ref[:, 0:1]
        c_prev = c_ref[...]
        n_prev = n_ref[...]

        lmat = jnp.where(causal, fc - fr + lir, -jnp.inf)
        m_inter = fc + m_prev
        m = jnp.maximum(jnp.max(lmat, axis=-1, keepdims=True), m_inter)
        qb = q.astype(BF16)
        s_mat = _dot_nt(qb, k.astype(BF16)) * jnp.exp(lmat - m)
        w_inter = jnp.exp(m_inter - m)
        num = jnp.dot(s_mat.astype(BF16), v, preferred_element_type=F32)
        num += w_inter * jnp.dot(qb, c_prev.astype(BF16), preferred_element_type=F32)
        den = jnp.sum(s_mat, axis=-1, keepdims=True)
        den += w_inter * jnp.sum(q * n_prev, axis=-1, keepdims=True)
        hid = num / jnp.maximum(jnp.abs(den), jnp.exp(-m))

        g = jnp.max(f_last - fr + lir, axis=-1, keepdims=True)
        kw = k * jnp.exp(f_last - fc + lic - g)
        d_c = _dot_tn(kw.astype(BF16), v)
        d_n = jnp.sum(kw, axis=0, keepdims=True)
        m_new = jnp.maximum(f_last + m_prev, g)
        a = jnp.exp(f_last + m_prev - m_new)
        bb = jnp.exp(g - m_new)
        c_ref[...] = a * c_prev + bb * d_c
        n_ref[...] = a * n_prev + bb * d_n
        m_ref[...] = jnp.broadcast_to(m_new, m_ref.shape)

        o = _sigmoid(mo_ref[rows, :]) * hid
        o_ref[rows, :] = _head_rmsnorm(o, mn_ref[...]).astype(o_ref.dtype)


def mlstm_branch(z, conv_w, conv_b, gate_bias, mn, *, t_blk=256):
    s = z.shape[0]
    return pl.pallas_call(
        functools.partial(_mlstm_kernel, t_blk=t_blk),
        grid=(N_HEADS, s // t_blk),
        in_specs=[
            pl.BlockSpec((t_blk, QK_DIM), lambda h, t: (t, COL_MQ + h)),
            pl.BlockSpec((t_blk, QK_DIM), lambda h, t: (t, COL_MK + h)),
            pl.BlockSpec((t_blk, V_DIM), lambda h, t: (t, COL_MV // 2 + h)),
            pl.BlockSpec((t_blk, V_DIM), lambda h, t: (t, COL_MO // 2 + h)),
            pl.BlockSpec((t_blk, LANE), lambda h, t: (t, COL_SMALL)),
            pl.BlockSpec((CONV_W, QK_DIM), lambda h, t: (0, h)),
            pl.BlockSpec((CONV_W, QK_DIM), lambda h, t: (0, N_HEADS + h)),
            pl.BlockSpec((1, QK_DIM), lambda h, t: (0, h)),
            pl.BlockSpec((1, QK_DIM), lambda h, t: (0, N_HEADS + h)),
            pl.BlockSpec((1, LANE), lambda h, t: (0, 0)),
            pl.BlockSpec((1, V_DIM), lambda h, t: (0, h)),
        ],
        out_specs=pl.BlockSpec((t_blk, V_DIM), lambda h, t: (t, h)),
        out_shape=jax.ShapeDtypeStruct((s, N_HEADS * V_DIM), BF16),
        scratch_shapes=[pltpu.VMEM((SUBLANE + t_blk, QK_DIM), F32),
                        pltpu.VMEM((SUBLANE + t_blk, QK_DIM), F32),
                        pltpu.VMEM((QK_DIM, V_DIM), F32),
                        pltpu.VMEM((1, QK_DIM), F32),
                        pltpu.VMEM((1, LANE), F32)],
        compiler_params=_cparams(("parallel", "arbitrary")),
        name="mlstm",
    )(z, z, z, z, z, conv_w, conv_w, conv_b, conv_b, gate_bias, mn)


def _memattn_kernel(q_ref, km_ref, vm_ref, o_ref):
    scale = V_DIM ** -0.5
    for h in range(N_HEADS):
        cols = slice(h * V_DIM, (h + 1) * V_DIM)
        q = q_ref[:, cols].astype(BF16)
        s = _dot_nt(q, km_ref[:, cols]) * scale
        e = jnp.exp(s - jnp.max(s, axis=-1, keepdims=True))
        p = e / jnp.sum(e, axis=-1, keepdims=True)
        o = jnp.dot(p.astype(BF16), vm_ref[:, cols], preferred_element_type=F32)
        o_ref[:, cols] = o.astype(o_ref.dtype)


def memattn_branch(z, km, vm, *, tm=512):
    s = z.shape[0]
    w = N_HEADS * V_DIM
    return pl.pallas_call(
        _memattn_kernel,
        grid=(s // tm,),
        in_specs=[pl.BlockSpec((tm, w), lambda i: (i, COL_XQ * LANE // w)),
                  pl.BlockSpec((MEM_LEN, w), lambda i: (0, 0)),
                  pl.BlockSpec((MEM_LEN, w), lambda i: (0, 0))],
        out_specs=pl.BlockSpec((tm, w), lambda i: (i, 0)),
        out_shape=jax.ShapeDtypeStruct((s, w), BF16),
        compiler_params=_cparams(("parallel",)),
        name="memattn",
    )(z, km, vm)


def _merge_kernel(o1_ref, o2_ref, o3_ref, gl_ref, wb_ref, wg_ref, bg_ref, y_ref):
    gl = gl_ref[...].astype(BF16)
    y = None
    for j, o_ref in enumerate((o1_ref, o2_ref, o3_ref)):
        gate = _sigmoid(jnp.dot(gl, wg_ref[j], preferred_element_type=F32)
                        + bg_ref[j:j + 1, :])
        term = gate * jnp.dot(o_ref[...], wb_ref[j], preferred_element_type=F32)
        y = term if y is None else y + term
    y_ref[...] = y.astype(y_ref.dtype)


def merge_branches(o1, o2, o3, z, wb, wg, bg, *, tm=1024, tn=512):
    s, w = o1.shape
    d = wb.shape[-1]
    o_spec = pl.BlockSpec((tm, w), lambda i, j: (i, 0))
    return pl.pallas_call(
        _merge_kernel,
        grid=(s // tm, d // tn),
        in_specs=[o_spec, o_spec, o_spec,
                  pl.BlockSpec((tm, GATE_RANK), lambda i, j: (i, COL_GATE * LANE // GATE_RANK)),
                  pl.BlockSpec((N_BRANCH, w, tn), lambda i, j: (0, 0, j)),
                  pl.BlockSpec((N_BRANCH, GATE_RANK, tn), lambda i, j: (0, 0, j)),
                  pl.BlockSpec((N_BRANCH, tn), lambda i, j: (0, j))],
        out_specs=pl.BlockSpec((tm, tn), lambda i, j: (i, j)),
        out_shape=jax.ShapeDtypeStruct((s, d), BF16),
        compiler_params=_cparams(("parallel", "parallel")),
        name="merge",
    )(o1, o2, o3, z, wb, wg, bg)


def _permute_w_in(w_in):
    sizes = (512, 512, 1024, 1024, GLA_RANK, 1024, 1024, 1024, N_HEADS, N_HEADS, 1024, GATE_RANK)
    offs = [0]
    for sz in sizes:
        offs.append(offs[-1] + sz)
    gq, gk, gv, gg, ga, mqk, mv, mo, mi, mf, xq, gate = (
        w_in[..., offs[i]:offs[i + 1]] for i in range(len(sizes)))
    lead = w_in.shape[:-1]
    small = jnp.concatenate(
        [ga, mi, mf, jnp.zeros(lead + (LANE - GLA_RANK - 2 * N_HEADS,), w_in.dtype)], axis=-1)
    pad = jnp.zeros(lead + ((IN_COLS - COL_SMALL - 1) * LANE,), w_in.dtype)
    return jnp.concatenate([gv, gg, mv, mo, xq, mqk, gq, gk, gate, small, pad],
                           axis=-1).astype(BF16)


def kernel(x, mem, norm_mix, w_in, w_gla_a, b_gla_a, gla_norm, conv_w, conv_b, b_ml_i, b_ml_f, ml_norm, mem_norm, w_mem_k, w_mem_v, w_branch, w_gate, b_gate, w_out, norm_ffn, w_ff1, w_ff2, final_norm):
    depth = w_in.shape[0]
    _, s, d = x.shape
    xs = x.reshape(s, d)

    w_in_p = _permute_w_in(w_in)
    wa_pad = jnp.zeros((depth, LANE, N_HEADS * QK_DIM), F32)
    wa_pad = wa_pad.at[:, SM_GA:SM_GA + GLA_RANK, :].set(w_gla_a).astype(BF16)
    gate_bias = jnp.zeros((depth, 1, LANE), F32)
    gate_bias = gate_bias.at[:, 0, SM_MI:SM_MI + N_HEADS].set(b_ml_i)
    gate_bias = gate_bias.at[:, 0, SM_MF:SM_MF + N_HEADS].set(b_ml_f)
    w_branch_b = w_branch.astype(BF16)
    w_gate_b = w_gate.astype(BF16)
    w_out_b = w_out.astype(BF16)
    w_ff1_b = w_ff1.astype(BF16)
    w_ff2_b = w_ff2.astype(BF16)

    memn = rmsnorm(mem.reshape(MEM_LEN, d), mem_norm, BF16, tm=MEM_LEN)

    for l in range(depth):
        h = rmsnorm(xs, norm_mix[l], BF16)
        z = matmul(h, w_in_p[l], tm=1024, tn=768, out_dtype=F32, name="in_proj")
        km = matmul(memn, w_mem_k[l], tm=MEM_LEN, tn=256, out_dtype=BF16, name="mem_k")
        vm = matmul(memn, w_mem_v[l], tm=MEM_LEN, tn=256, out_dtype=BF16, name="mem_v")
        o_gla = gla_branch(z, wa_pad[l], b_gla_a[l].reshape(1, -1), gla_norm[l].reshape(1, -1))
        o_ml = mlstm_branch(z, conv_w[l], conv_b[l].reshape(1, -1), gate_bias[l],
                            ml_norm[l].reshape(1, -1))
        o_mem = memattn_branch(z, km, vm)
        y = merge_branches(o_gla, o_ml, o_mem, z, w_branch_b[l], w_gate_b[l], b_gate[l])
        xs = matmul(y, w_out_b[l], tm=1024, tn=1024, out_dtype=F32, residual=xs,
                    name="out_proj")
        h = rmsnorm(xs, norm_ffn[l], BF16)
        hid = matmul(h, w_ff1_b[l], tm=1024, tn=1024, out_dtype=BF16, epilogue="relu2",
                     name="ffn_up")
        xs = matmul(hid, w_ff2_b[l], tm=1024, tn=1024, tk=2048, out_dtype=F32, residual=xs,
                    name="ffn_down")
    return rmsnorm(xs, final_norm, F32).reshape(x.shape)
```

```python
import functools

import jax
import jax.numpy as jnp
from jax import lax
from jax.experimental import pallas as pl
from jax.experimental.pallas import tpu as pltpu

F32 = jnp.float32
BF16 = jnp.bfloat16

EPS = 1e-6
CHUNK = 64
N_HEADS = 4
QK_DIM = 128
V_DIM = 256
QK_W = N_HEADS * QK_DIM
V_W = N_HEADS * V_DIM
GLA_RANK = 16
GLA_TAU = 16.0
CONV_W = 4
MEM_LEN = 256
GATE_RANK = 256
N_BRANCH = 3

LANE = 128
SUBLANE = 8
VMEM_LIMIT = 56 * 1024 * 1024

COL_GV, COL_GG, COL_MV, COL_MO, COL_XQ = 0, 8, 16, 24, 32
COL_MQK, COL_GQ, COL_GK = 40, 48, 52
COL_GATE, COL_SMALL, IN_COLS = 56, 58, 60
SM_GA, SM_MI, SM_MF = 0, 16, 20


def _cparams(sem):
    return pltpu.CompilerParams(dimension_semantics=sem, vmem_limit_bytes=VMEM_LIMIT)


def _col_block(col, width):
    return col * LANE // width


def _rmsnorm_kernel(x_ref, g_ref, o_ref):
    x = x_ref[...]
    ms = jnp.mean(x * x, axis=-1, keepdims=True)
    o_ref[...] = ((x * lax.rsqrt(ms + EPS)) * g_ref[...]).astype(o_ref.dtype)


def rmsnorm(x, g, out_dtype, tm=256):
    m, d = x.shape
    tm = min(tm, m)
    return pl.pallas_call(
        _rmsnorm_kernel,
        grid=(m // tm,),
        in_specs=[pl.BlockSpec((tm, d), lambda i: (i, 0)),
                  pl.BlockSpec((1, d), lambda i: (0, 0))],
        out_specs=pl.BlockSpec((tm, d), lambda i: (i, 0)),
        out_shape=jax.ShapeDtypeStruct((m, d), out_dtype),
        compiler_params=_cparams(("parallel",)),
        name="rmsnorm",
    )(x, g.reshape(1, d))


def _apply_epilogue(acc, epilogue):
    if epilogue == "relu2":
        return jnp.square(jnp.maximum(acc, 0.0))
    return acc


def _mm_rows_kernel(a_ref, b_ref, o_ref, *, epilogue):
    acc = jnp.dot(a_ref[...], b_ref[...], preferred_element_type=F32)
    o_ref[...] = _apply_epilogue(acc, epilogue).astype(o_ref.dtype)


def matmul_rows(a, b, layer, *, tm, tn, out_dtype, epilogue=None, name):
    m, kdim = a.shape
    n = b.shape[-1]
    tm = min(tm, m)
    return pl.pallas_call(
        functools.partial(_mm_rows_kernel, epilogue=epilogue),
        grid=(m // tm, n // tn),
        in_specs=[pl.BlockSpec((tm, kdim), lambda i, j: (i, 0)),
                  pl.BlockSpec((None, kdim, tn), lambda i, j: (layer, 0, j))],
        out_specs=pl.BlockSpec((tm, tn), lambda i, j: (i, j)),
        out_shape=jax.ShapeDtypeStruct((m, n), out_dtype),
        compiler_params=_cparams(("parallel", "parallel")),
        name=name,
    )(a, b)


def _mm_cols_kernel(a_ref, b_ref, o_ref, wb_ref, *, epilogue):
    @pl.when(pl.program_id(1) == 0)
    def _():
        wb_ref[...] = b_ref[...].astype(BF16)

    acc = jnp.dot(a_ref[...].astype(BF16), wb_ref[...], preferred_element_type=F32)
    o_ref[...] = _apply_epilogue(acc, epilogue).astype(o_ref.dtype)


def matmul_cols(a, b, layer, *, tm, tn, out_dtype, epilogue=None, name):
    m, kdim = a.shape
    n = b.shape[-1]
    tm = min(tm, m)
    return pl.pallas_call(
        functools.partial(_mm_cols_kernel, epilogue=epilogue),
        grid=(n // tn, m // tm),
        in_specs=[pl.BlockSpec((tm, kdim), lambda j, i: (i, 0)),
                  pl.BlockSpec((None, kdim, tn), lambda j, i: (layer, 0, j))],
        out_specs=pl.BlockSpec((tm, tn), lambda j, i: (i, j)),
        out_shape=jax.ShapeDtypeStruct((m, n), out_dtype),
        scratch_shapes=[pltpu.VMEM((kdim, tn), BF16)],
        compiler_params=_cparams(("parallel", "arbitrary")),
        name=name,
    )(a, b)


def _mm_acc_kernel(a_ref, b_ref, r_ref, o_ref):
    @pl.when(pl.program_id(2) == 0)
    def _():
        o_ref[...] = r_ref[...]

    o_ref[...] = jnp.dot(a_ref[...], b_ref[...], preferred_element_type=F32) + o_ref[...]


def matmul_acc(a, b, layer, residual, *, tm, tn, tk, name):
    m, kdim = a.shape
    n = b.shape[-1]
    tm = min(tm, m)
    return pl.pallas_call(
        _mm_acc_kernel,
        grid=(m // tm, n // tn, kdim // tk),
        in_specs=[pl.BlockSpec((tm, tk), lambda i, j, k: (i, k)),
                  pl.BlockSpec((None, tk, tn), lambda i, j, k: (layer, k, j)),
                  pl.BlockSpec((tm, tn), lambda i, j, k: (i, j))],
        out_specs=pl.BlockSpec((tm, tn), lambda i, j, k: (i, j)),
        out_shape=jax.ShapeDtypeStruct((m, n), F32),
        compiler_params=_cparams(("parallel", "parallel", "arbitrary")),
        name=name,
    )(a, b, residual)


def _log_sigmoid(x):
    return jnp.minimum(x, 0.0) - jnp.log1p(jnp.exp(-jnp.abs(x)))


def _sigmoid(x):
    return 1.0 / (1.0 + jnp.exp(-x))


def chunk_tri(t):
    r = lax.broadcasted_iota(jnp.int32, (t, t), 0)
    c = lax.broadcasted_iota(jnp.int32, (t, t), 1)
    same = (r // CHUNK) == (c // CHUNK)
    return jnp.logical_and(same, c <= r).astype(BF16)


def _chunk_cumsum(x, tri):
    hi = x.astype(BF16)
    r1 = x - hi.astype(F32)
    mid = r1.astype(BF16)
    lo = (r1 - mid.astype(F32)).astype(BF16)
    out = jnp.dot(tri, hi, preferred_element_type=F32)
    out += jnp.dot(tri, mid, preferred_element_type=F32)
    out += jnp.dot(tri, lo, preferred_element_type=F32)
    return out


def _dot_nt(a, b):
    return lax.dot_general(a, b, (((1,), (1,)), ((), ())), preferred_element_type=F32)


def _dot_tn(a, b):
    return lax.dot_general(a, b, (((0,), (0,)), ((), ())), preferred_element_type=F32)


def _causal_mask():
    r = lax.broadcasted_iota(jnp.int32, (CHUNK, CHUNK), 0)
    c = lax.broadcasted_iota(jnp.int32, (CHUNK, CHUNK), 1)
    return c <= r


def _head_rmsnorm(o, g):
    ms = jnp.mean(o * o, axis=-1, keepdims=True)
    return (o * lax.rsqrt(ms + EPS)) * g


def _gla_kernel(q_ref, k_ref, v_ref, gg_ref, sm_ref, tri_ref, wa_ref, ba_ref, gn_ref,
                o_ref, st_ref, *, t_blk):
    @pl.when(pl.program_id(0) == 0)
    def _():
        st_ref[...] = jnp.zeros_like(st_ref)

    pre = jnp.dot(sm_ref[...].astype(BF16), wa_ref[...],
                  preferred_element_type=F32) + ba_ref[...]
    log_a = _log_sigmoid(pre) / GLA_TAU
    b_all = _chunk_cumsum(log_a, tri_ref[...])
    causal = _causal_mask()
    scale = QK_DIM ** -0.5
    states = [st_ref[h] for h in range(N_HEADS)]
    for c in range(t_blk // CHUNK):
        rows = slice(c * CHUNK, (c + 1) * CHUNK)
        for h in range(N_HEADS):
            kc = slice(h * QK_DIM, (h + 1) * QK_DIM)
            vc = slice(h * V_DIM, (h + 1) * V_DIM)
            b = b_all[rows, kc]
            b_last = b[CHUNK - 1:CHUNK]
            q = q_ref[rows, kc] * scale
            k = k_ref[rows, kc]
            v = v_ref[rows, vc].astype(BF16)
            q_d = (q * jnp.exp(b)).astype(BF16)
            k_d = (k * jnp.exp(-b)).astype(BF16)
            k_s = (k * jnp.exp(b_last - b)).astype(BF16)
            att = jnp.where(causal, _dot_nt(q_d, k_d), 0.0)
            st = states[h]
            o = jnp.dot(att.astype(BF16), v, preferred_element_type=F32)
            o += _dot_nt(q_d, st.astype(BF16))
            states[h] = st * jnp.exp(b_last) + _dot_tn(v, k_s)
            gg = gg_ref[rows, vc]
            o = _head_rmsnorm(o, gn_ref[:, vc]) * (gg * _sigmoid(gg))
            o_ref[rows, vc] = o.astype(o_ref.dtype)
    for h in range(N_HEADS):
        st_ref[h] = states[h]


def gla_branch(z, tri, wa_pad, ba, gn, layer, *, t_blk=256):
    s = z.shape[0]
    return pl.pallas_call(
        functools.partial(_gla_kernel, t_blk=t_blk),
        grid=(s // t_blk,),
        in_specs=[
            pl.BlockSpec((t_blk, QK_W), lambda t: (t, _col_block(COL_GQ, QK_W))),
            pl.BlockSpec((t_blk, QK_W), lambda t: (t, _col_block(COL_GK, QK_W))),
            pl.BlockSpec((t_blk, V_W), lambda t: (t, _col_block(COL_GV, V_W))),
            pl.BlockSpec((t_blk, V_W), lambda t: (t, _col_block(COL_GG, V_W))),
            pl.BlockSpec((t_blk, LANE), lambda t: (t, COL_SMALL)),
            pl.BlockSpec((t_blk, t_blk), lambda t: (0, 0)),
            pl.BlockSpec((None, LANE, QK_W), lambda t: (layer, 0, 0)),
            pl.BlockSpec((None, 1, QK_W), lambda t: (layer, 0, 0)),
            pl.BlockSpec((None, 1, V_W), lambda t: (layer, 0, 0)),
        ],
        out_specs=pl.BlockSpec((t_blk, V_W), lambda t: (t, 0)),
        out_shape=jax.ShapeDtypeStruct((s, V_W), BF16),
        scratch_shapes=[pltpu.VMEM((N_HEADS, V_DIM, QK_DIM), F32)],
        compiler_params=_cparams(("arbitrary",)),
        name="gla",
    )(z, z, z, z, z, tri, wa_pad, ba, gn)


def _mlstm_kernel(qk_ref, v_ref, mo_ref, sm_ref, tri_ref, cw_ref, cb_ref, gb_ref, mn_ref,
                  o_ref, u_ref, c_ref, n_ref, m_ref, *, t_blk):
    @pl.when(pl.program_id(0) == 0)
    def _():
        u_ref[0:SUBLANE, :] = jnp.zeros((SUBLANE, 2 * QK_W), F32)
        c_ref[...] = jnp.zeros_like(c_ref)
        n_ref[...] = jnp.zeros_like(n_ref)
        m_ref[...] = jnp.full(m_ref.shape, -1e30, F32)

    u_ref[SUBLANE:SUBLANE + t_blk, :] = qk_ref[...]
    y = cb_ref[...]
    for j in range(CONV_W):
        off = SUBLANE - (CONV_W - 1) + j
        y = y + cw_ref[j:j + 1, :] * u_ref[off:off + t_blk, :]
    u_ref[0:SUBLANE, :] = u_ref[t_blk:t_blk + SUBLANE, :]
    qk_all = y * _sigmoid(y)

    pre = sm_ref[...] + gb_ref[...]
    f_cum = _chunk_cumsum(_log_sigmoid(pre), tri_ref[...])
    lane = lax.broadcasted_iota(jnp.int32, (t_blk, LANE), 1)
    is_f = jnp.logical_and(lane >= SM_MF, lane < SM_MF + N_HEADS)
    cols = jnp.where(is_f, f_cum, pre)
    rows_t = cols.T

    causal = _causal_mask()
    k_scale = QK_DIM ** -0.5
    c_st = [c_ref[h] for h in range(N_HEADS)]
    n_st = [n_ref[h] for h in range(N_HEADS)]
    m_st = [m_ref[h][:, 0:1] for h in range(N_HEADS)]
    for c in range(t_blk // CHUNK):
        rows = slice(c * CHUNK, (c + 1) * CHUNK)
        for h in range(N_HEADS):
            vc = slice(h * V_DIM, (h + 1) * V_DIM)
            q = qk_all[rows, h * QK_DIM:(h + 1) * QK_DIM]
            k = qk_all[rows, QK_W + h * QK_DIM:QK_W + (h + 1) * QK_DIM] * k_scale
            v = v_ref[rows, vc].astype(BF16)
            lic = cols[rows, SM_MI + h:SM_MI + h + 1]
            fc = cols[rows, SM_MF + h:SM_MF + h + 1]
            lir = rows_t[SM_MI + h:SM_MI + h + 1, rows]
            fr = rows_t[SM_MF + h:SM_MF + h + 1, rows]
            f_last = fr[:, CHUNK - 1:CHUNK]
            m_prev, c_prev, n_prev = m_st[h], c_st[h], n_st[h]

            lmat = jnp.where(causal, fc - fr + lir, -jnp.inf)
            m_inter = fc + m_prev
            m = jnp.maximum(jnp.max(lmat, axis=-1, keepdims=True), m_inter)
            qb = q.astype(BF16)
            s_mat = _dot_nt(qb, k.astype(BF16)) * jnp.exp(lmat - m)
            w_inter = jnp.exp(m_inter - m)
            num = jnp.dot(s_mat.astype(BF16), v, preferred_element_type=F32)
            num += w_inter * jnp.dot(qb, c_prev.astype(BF16), preferred_element_type=F32)
            den = jnp.sum(s_mat, axis=-1, keepdims=True)
            den += w_inter * jnp.sum(q * n_prev, axis=-1, keepdims=True)
            hid = num / jnp.maximum(jnp.abs(den), jnp.exp(-m))

            g = jnp.max(f_last - fr + lir, axis=-1, keepdims=True)
            kw = k * jnp.exp(f_last - fc + lic - g)
            d_c = _dot_tn(kw.astype(BF16), v)
            d_n = jnp.sum(kw, axis=0, keepdims=True)
            m_new = jnp.maximum(f_last + m_prev, g)
            a = jnp.exp(f_last + m_prev - m_new)
            bb = jnp.exp(g - m_new)
            c_st[h] = a * c_prev + bb * d_c
            n_st[h] = a * n_prev + bb * d_n
            m_st[h] = m_new

            o = _sigmoid(mo_ref[rows, vc]) * hid
            o_ref[rows, vc] = _head_rmsnorm(o, mn_ref[:, vc]).astype(o_ref.dtype)
    for h in range(N_HEADS):
        c_ref[h] = c_st[h]
        n_ref[h] = n_st[h]
        m_ref[h] = jnp.broadcast_to(m_st[h], (1, LANE))


def mlstm_branch(z, tri, conv_w, conv_b, gate_bias, mn, layer, *, t_blk=256):
    s = z.shape[0]
    return pl.pallas_call(
        functools.partial(_mlstm_kernel, t_blk=t_blk),
        grid=(s // t_blk,),
        in_specs=[
            pl.BlockSpec((t_blk, 2 * QK_W), lambda t: (t, _col_block(COL_MQK, 2 * QK_W))),
            pl.BlockSpec((t_blk, V_W), lambda t: (t, _col_block(COL_MV, V_W))),
            pl.BlockSpec((t_blk, V_W), lambda t: (t, _col_block(COL_MO, V_W))),
            pl.BlockSpec((t_blk, LANE), lambda t: (t, COL_SMALL)),
            pl.BlockSpec((t_blk, t_blk), lambda t: (0, 0)),
            pl.BlockSpec((None, CONV_W, 2 * QK_W), lambda t: (layer, 0, 0)),
            pl.BlockSpec((None, 1, 2 * QK_W), lambda t: (layer, 0, 0)),
            pl.BlockSpec((None, 1, LANE), lambda t: (layer, 0, 0)),
            pl.BlockSpec((None, 1, V_W), lambda t: (layer, 0, 0)),
        ],
        out_specs=pl.BlockSpec((t_blk, V_W), lambda t: (t, 0)),
        out_shape=jax.ShapeDtypeStruct((s, V_W), BF16),
        scratch_shapes=[pltpu.VMEM((SUBLANE + t_blk, 2 * QK_W), F32),
                        pltpu.VMEM((N_HEADS, QK_DIM, V_DIM), F32),
                        pltpu.VMEM((N_HEADS, 1, QK_DIM), F32),
                        pltpu.VMEM((N_HEADS, 1, LANE), F32)],
        compiler_params=_cparams(("arbitrary",)),
        name="mlstm",
    )(z, z, z, z, tri, conv_w, conv_b, gate_bias, mn)


def _memattn_kernel(q_ref, km_ref, vm_ref, o_ref):
    scale = V_DIM ** -0.5
    for h in range(N_HEADS):
        cols = slice(h * V_DIM, (h + 1) * V_DIM)
        q = q_ref[:, cols].astype(BF16)
        s = _dot_nt(q, km_ref[:, cols]) * scale
        e = jnp.exp(s - jnp.max(s, axis=-1, keepdims=True))
        p = e / jnp.sum(e, axis=-1, keepdims=True)
        o = jnp.dot(p.astype(BF16), vm_ref[:, cols], preferred_element_type=F32)
        o_ref[:, cols] = o.astype(o_ref.dtype)


def memattn_branch(z, km, vm, *, tm=512):
    s = z.shape[0]
    tm = min(tm, s)
    return pl.pallas_call(
        _memattn_kernel,
        grid=(s // tm,),
        in_specs=[pl.BlockSpec((tm, V_W), lambda i: (i, _col_block(COL_XQ, V_W))),
                  pl.BlockSpec((MEM_LEN, V_W), lambda i: (0, 0)),
                  pl.BlockSpec((MEM_LEN, V_W), lambda i: (0, 0))],
        out_specs=pl.BlockSpec((tm, V_W), lambda i: (i, 0)),
        out_shape=jax.ShapeDtypeStruct((s, V_W), BF16),
        compiler_params=_cparams(("parallel",)),
        name="memattn",
    )(z, km, vm)


def _merge_kernel(o1_ref, o2_ref, o3_ref, gl_ref, wb_ref, wg_ref, bg_ref, y_ref,
                  wbb_ref, wgb_ref):
    @pl.when(pl.program_id(1) == 0)
    def _():
        wbb_ref[...] = wb_ref[...].astype(BF16)
        wgb_ref[...] = wg_ref[...].astype(BF16)

    gl = gl_ref[...].astype(BF16)
    y = None
    for j, o_ref in enumerate((o1_ref, o2_ref, o3_ref)):
        gate = _sigmoid(jnp.dot(gl, wgb_ref[j], preferred_element_type=F32)
                        + bg_ref[j:j + 1, :])
        term = gate * jnp.dot(o_ref[...], wbb_ref[j], preferred_element_type=F32)
        y = term if y is None else y + term
    y_ref[...] = y.astype(y_ref.dtype)


def merge_branches(o1, o2, o3, z, wb, wg, bg, layer, *, tm=1024, tn=512):
    s, w = o1.shape
    d = wb.shape[-1]
    tm = min(tm, s)
    o_spec = pl.BlockSpec((tm, w), lambda j, i: (i, 0))
    return pl.pallas_call(
        _merge_kernel,
        grid=(d // tn, s // tm),
        in_specs=[o_spec, o_spec, o_spec,
                  pl.BlockSpec((tm, GATE_RANK), lambda j, i: (i, _col_block(COL_GATE, GATE_RANK))),
                  pl.BlockSpec((None, N_BRANCH, w, tn), lambda j, i: (layer, 0, 0, j)),
                  pl.BlockSpec((None, N_BRANCH, GATE_RANK, tn), lambda j, i: (layer, 0, 0, j)),
                  pl.BlockSpec((None, N_BRANCH, tn), lambda j, i: (layer, 0, j))],
        out_specs=pl.BlockSpec((tm, tn), lambda j, i: (i, j)),
        out_shape=jax.ShapeDtypeStruct((s, d), BF16),
        scratch_shapes=[pltpu.VMEM((N_BRANCH, w, tn), BF16),
                        pltpu.VMEM((N_BRANCH, GATE_RANK, tn), BF16)],
        compiler_params=_cparams(("parallel", "arbitrary")),
        name="merge",
    )(o1, o2, o3, z, wb, wg, bg)


def _permute_w_in(w_in):
    sizes = (QK_W, QK_W, V_W, V_W, GLA_RANK, 2 * QK_W, V_W, V_W, N_HEADS, N_HEADS, V_W, GATE_RANK)
    offs = [0]
    for sz in sizes:
        offs.append(offs[-1] + sz)
    gq, gk, gv, gg, ga, mqk, mv, mo, mi, mf, xq, gate = (
        w_in[..., offs[i]:offs[i + 1]].astype(BF16) for i in range(len(sizes)))
    lead = w_in.shape[:-1]
    small_pad = jnp.zeros(lead + (LANE - GLA_RANK - 2 * N_HEADS,), BF16)
    pad = jnp.zeros(lead + ((IN_COLS - COL_SMALL - 1) * LANE,), BF16)
    return jnp.concatenate([gv, gg, mv, mo, xq, mqk, gq, gk, gate, ga, mi, mf, small_pad, pad],
                           axis=-1)


def kernel(x, mem, norm_mix, w_in, w_gla_a, b_gla_a, gla_norm, conv_w, conv_b, b_ml_i, b_ml_f, ml_norm, mem_norm, w_mem_k, w_mem_v, w_branch, w_gate, b_gate, w_out, norm_ffn, w_ff1, w_ff2, final_norm):
    depth = w_in.shape[0]
    _, s, d = x.shape
    xs = x.reshape(s, d)
    t_blk = min(256, s)

    w_in_p = _permute_w_in(w_in)
    wa_pad = jnp.zeros((depth, LANE, QK_W), F32)
    wa_pad = wa_pad.at[:, SM_GA:SM_GA + GLA_RANK, :].set(w_gla_a).astype(BF16)
    gate_bias = jnp.zeros((depth, 1, LANE), F32)
    gate_bias = gate_bias.at[:, 0, SM_MI:SM_MI + N_HEADS].set(b_ml_i)
    gate_bias = gate_bias.at[:, 0, SM_MF:SM_MF + N_HEADS].set(b_ml_f)
    w_out_b = w_out.astype(BF16)
    w_ff2_b = w_ff2.astype(BF16)
    tri = chunk_tri(t_blk)
    b_gla_a3 = b_gla_a.reshape(depth, 1, QK_W)
    gla_norm3 = gla_norm.reshape(depth, 1, V_W)
    conv_b3 = conv_b.reshape(depth, 1, 2 * QK_W)
    ml_norm3 = ml_norm.reshape(depth, 1, V_W)

    memn = rmsnorm(mem.reshape(MEM_LEN, d), mem_norm, BF16, tm=MEM_LEN)

    for l in range(depth):
        h = rmsnorm(xs, norm_mix[l], BF16)
        z = matmul_rows(h, w_in_p, l, tm=1024, tn=768, out_dtype=F32, name="in_proj")
        km = matmul_cols(memn, w_mem_k, l, tm=MEM_LEN, tn=256, out_dtype=BF16, name="mem_k")
        vm = matmul_cols(memn, w_mem_v, l, tm=MEM_LEN, tn=256, out_dtype=BF16, name="mem_v")
        o_gla = gla_branch(z, tri, wa_pad, b_gla_a3, gla_norm3, l, t_blk=t_blk)
        o_ml = mlstm_branch(z, tri, conv_w, conv_b3, gate_bias, ml_norm3, l, t_blk=t_blk)
        o_mem = memattn_branch(z, km, vm)
        y = merge_branches(o_gla, o_ml, o_mem, z, w_branch, w_gate, b_gate, l)
        xs = matmul_acc(y, w_out_b, l, xs, tm=1024, tn=1024, tk=d, name="out_proj")
        h = rmsnorm(xs, norm_ffn[l], BF16)
        hid = matmul_cols(h, w_ff1, l, tm=1024, tn=512, out_dtype=BF16, epilogue="relu2",
                          name="ffn_up")
        xs = matmul_acc(hid, w_ff2_b, l, xs, tm=1024, tn=1024, tk=4096, name="ffn_down")
    return rmsnorm(xs, final_norm, F32).reshape(x.shape)
```

```python
import functools

import jax
import jax.numpy as jnp
from jax import lax
from jax.experimental import pallas as pl
from jax.experimental.pallas import tpu as pltpu

F32 = jnp.float32
BF16 = jnp.bfloat16

EPS = 1e-6
CHUNK = 64
N_HEADS = 4
QK_DIM = 128
V_DIM = 256
QK_W = N_HEADS * QK_DIM
V_W = N_HEADS * V_DIM
GLA_RANK = 16
GLA_TAU = 16.0
CONV_W = 4
MEM_LEN = 256
GATE_RANK = 256
N_BRANCH = 3

LANE = 128
SUBLANE = 8
VMEM_LIMIT = 56 * 1024 * 1024

COL_GV, COL_GG, COL_MV, COL_MO, COL_XQ = 0, 8, 16, 24, 32
COL_MQK, COL_GQ, COL_GK = 40, 48, 52
COL_GATE, COL_SMALL, IN_COLS = 56, 58, 60
SM_GA, SM_MI, SM_MF = 0, 16, 20


def _cparams(sem):
    return pltpu.CompilerParams(dimension_semantics=sem, vmem_limit_bytes=VMEM_LIMIT)


def _col_block(col, width):
    return col * LANE // width


def _rmsnorm_kernel(x_ref, g_ref, o_ref):
    x = x_ref[...]
    ms = jnp.mean(x * x, axis=-1, keepdims=True)
    o_ref[...] = ((x * lax.rsqrt(ms + EPS)) * g_ref[...]).astype(o_ref.dtype)


def rmsnorm(x, g, out_dtype, tm=256):
    m, d = x.shape
    tm = min(tm, m)
    return pl.pallas_call(
        _rmsnorm_kernel,
        grid=(m // tm,),
        in_specs=[pl.BlockSpec((tm, d), lambda i: (i, 0)),
                  pl.BlockSpec((1, d), lambda i: (0, 0))],
        out_specs=pl.BlockSpec((tm, d), lambda i: (i, 0)),
        out_shape=jax.ShapeDtypeStruct((m, d), out_dtype),
        compiler_params=_cparams(("parallel",)),
        name="rmsnorm",
    )(x, g.reshape(1, d))


def _apply_epilogue(acc, epilogue):
    if epilogue == "relu2":
        return jnp.square(jnp.maximum(acc, 0.0))
    return acc


def _mm_rows_kernel(a_ref, b_ref, o_ref, *, epilogue):
    acc = jnp.dot(a_ref[...], b_ref[...], preferred_element_type=F32)
    o_ref[...] = _apply_epilogue(acc, epilogue).astype(o_ref.dtype)


def matmul_rows(a, b, layer, *, tm, tn, out_dtype, epilogue=None, name):
    m, kdim = a.shape
    n = b.shape[-1]
    tm = min(tm, m)
    return pl.pallas_call(
        functools.partial(_mm_rows_kernel, epilogue=epilogue),
        grid=(m // tm, n // tn),
        in_specs=[pl.BlockSpec((tm, kdim), lambda i, j: (i, 0)),
                  pl.BlockSpec((None, kdim, tn), lambda i, j: (layer, 0, j))],
        out_specs=pl.BlockSpec((tm, tn), lambda i, j: (i, j)),
        out_shape=jax.ShapeDtypeStruct((m, n), out_dtype),
        compiler_params=_cparams(("parallel", "parallel")),
        name=name,
    )(a, b)


def _mm_cols_kernel(a_ref, b_ref, o_ref, wb_ref, *, epilogue):
    @pl.when(pl.program_id(1) == 0)
    def _():
        wb_ref[...] = b_ref[...].astype(BF16)

    acc = jnp.dot(a_ref[...].astype(BF16), wb_ref[...], preferred_element_type=F32)
    o_ref[...] = _apply_epilogue(acc, epilogue).astype(o_ref.dtype)


def matmul_cols(a, b, layer, *, tm, tn, out_dtype, epilogue=None, name):
    m, kdim = a.shape
    n = b.shape[-1]
    tm = min(tm, m)
    return pl.pallas_call(
        functools.partial(_mm_cols_kernel, epilogue=epilogue),
        grid=(n // tn, m // tm),
        in_specs=[pl.BlockSpec((tm, kdim), lambda j, i: (i, 0)),
                  pl.BlockSpec((None, kdim, tn), lambda j, i: (layer, 0, j))],
        out_specs=pl.BlockSpec((tm, tn), lambda j, i: (i, j)),
        out_shape=jax.ShapeDtypeStruct((m, n), out_dtype),
        scratch_shapes=[pltpu.VMEM((kdim, tn), BF16)],
        compiler_params=_cparams(("parallel", "arbitrary")),
        name=name,
    )(a, b)


def _mm_acc_kernel(a_ref, b_ref, r_ref, o_ref):
    @pl.when(pl.program_id(2) == 0)
    def _():
        o_ref[...] = r_ref[...]

    o_ref[...] = jnp.dot(a_ref[...], b_ref[...], preferred_element_type=F32) + o_ref[...]


def matmul_acc(a, b, layer, residual, *, tm, tn, tk, name):
    m, kdim = a.shape
    n = b.shape[-1]
    tm = min(tm, m)
    return pl.pallas_call(
        _mm_acc_kernel,
        grid=(m // tm, n // tn, kdim // tk),
        in_specs=[pl.BlockSpec((tm, tk), lambda i, j, k: (i, k)),
                  pl.BlockSpec((None, tk, tn), lambda i, j, k: (layer, k, j)),
                  pl.BlockSpec((tm, tn), lambda i, j, k: (i, j))],
        out_specs=pl.BlockSpec((tm, tn), lambda i, j, k: (i, j)),
        out_shape=jax.ShapeDtypeStruct((m, n), F32),
        compiler_params=_cparams(("parallel", "parallel", "arbitrary")),
        name=name,
    )(a, b, residual)


def _log_sigmoid(x):
    return jnp.minimum(x, 0.0) - jnp.log1p(jnp.exp(-jnp.abs(x)))


def _sigmoid(x):
    return 1.0 / (1.0 + jnp.exp(-x))


def chunk_tri(t):
    r = lax.broadcasted_iota(jnp.int32, (t, t), 0)
    c = lax.broadcasted_iota(jnp.int32, (t, t), 1)
    same = (r // CHUNK) == (c // CHUNK)
    return jnp.logical_and(same, c <= r).astype(BF16)


def _chunk_cumsum(x, tri):
    hi = x.astype(BF16)
    r1 = x - hi.astype(F32)
    mid = r1.astype(BF16)
    lo = (r1 - mid.astype(F32)).astype(BF16)
    out = jnp.dot(tri, hi, preferred_element_type=F32)
    out += jnp.dot(tri, mid, preferred_element_type=F32)
    out += jnp.dot(tri, lo, preferred_element_type=F32)
    return out


def _split_dot(x, sel):
    hi = x.astype(BF16)
    r1 = x - hi.astype(F32)
    mid = r1.astype(BF16)
    lo = (r1 - mid.astype(F32)).astype(BF16)
    out = jnp.dot(hi, sel, preferred_element_type=F32)
    out += jnp.dot(mid, sel, preferred_element_type=F32)
    out += jnp.dot(lo, sel, preferred_element_type=F32)
    return out


def gate_replicator():
    src = lax.broadcasted_iota(jnp.int32, (LANE, 2 * N_HEADS * LANE), 0)
    grp = lax.broadcasted_iota(jnp.int32, (LANE, 2 * N_HEADS * LANE), 1) // LANE
    want = jnp.where(grp % 2 == 0, SM_MI, SM_MF) + grp // 2
    return (src == want).astype(BF16)


def _dot_nt(a, b):
    return lax.dot_general(a, b, (((1,), (1,)), ((), ())), preferred_element_type=F32)


def _dot_tn(a, b):
    return lax.dot_general(a, b, (((0,), (0,)), ((), ())), preferred_element_type=F32)


def _head_rmsnorm(o, g):
    ms = jnp.mean(o * o, axis=-1, keepdims=True)
    return (o * lax.rsqrt(ms + EPS)) * g


def _gla_kernel(q_ref, k_ref, v_ref, gg_ref, sm_ref, tri_ref, mask_ref, wa_ref, ba_ref, gn_ref,
                o_ref, st_ref, *, t_blk):
    @pl.when(pl.program_id(0) == 0)
    def _():
        st_ref[...] = jnp.zeros_like(st_ref)

    pre = jnp.dot(sm_ref[...].astype(BF16), wa_ref[...],
                  preferred_element_type=F32) + ba_ref[...]
    log_a = _log_sigmoid(pre) / GLA_TAU
    b_all = _chunk_cumsum(log_a, tri_ref[...])
    allowed = mask_ref[...] > 0.5
    scale = QK_DIM ** -0.5
    n_chunks = t_blk // CHUNK
    chunk_rows = [slice(c * CHUNK, (c + 1) * CHUNK) for c in range(n_chunks)]
    heads = range(N_HEADS)
    vcs = [slice(h * V_DIM, (h + 1) * V_DIM) for h in heads]

    q_d, o_intra, d_st, decay = [], [], [], []
    for h in heads:
        kc = slice(h * QK_DIM, (h + 1) * QK_DIM)
        b = b_all[:, kc]
        b_last = [b[r][CHUNK - 1:CHUNK] for r in chunk_rows]
        b_last_rows = jnp.concatenate(
            [jnp.broadcast_to(bl, (CHUNK, QK_DIM)) for bl in b_last], axis=0)
        k = k_ref[:, kc]
        v = v_ref[:, vcs[h]].astype(BF16)
        qd = ((q_ref[:, kc] * scale) * jnp.exp(b)).astype(BF16)
        k_d = (k * jnp.exp(-b)).astype(BF16)
        k_s = (k * jnp.exp(b_last_rows - b)).astype(BF16)
        att = jnp.where(allowed, _dot_nt(qd, k_d), 0.0)
        q_d.append(qd)
        o_intra.append(jnp.dot(att.astype(BF16), v, preferred_element_type=F32))
        d_st.append([_dot_tn(v[r], k_s[r]) for r in chunk_rows])
        decay.append([jnp.exp(bl) for bl in b_last])

    states = [st_ref[h] for h in heads]
    inter = [[] for _ in heads]
    for c, r in enumerate(chunk_rows):
        for h in heads:
            inter[h].append(_dot_nt(q_d[h][r], states[h].astype(BF16)))
            states[h] = states[h] * decay[h][c] + d_st[h][c]
    for h in heads:
        st_ref[h] = states[h]

    for h in heads:
        o = o_intra[h] + jnp.concatenate(inter[h], axis=0)
        gg = gg_ref[:, vcs[h]]
        o = _head_rmsnorm(o, gn_ref[:, vcs[h]]) * (gg * _sigmoid(gg))
        o_ref[:, vcs[h]] = o.astype(o_ref.dtype)


def gla_branch(z, tri, mask, wa_pad, ba, gn, layer, *, t_blk=256):
    s = z.shape[0]
    return pl.pallas_call(
        functools.partial(_gla_kernel, t_blk=t_blk),
        grid=(s // t_blk,),
        in_specs=[
            pl.BlockSpec((t_blk, QK_W), lambda t: (t, _col_block(COL_GQ, QK_W))),
            pl.BlockSpec((t_blk, QK_W), lambda t: (t, _col_block(COL_GK, QK_W))),
            pl.BlockSpec((t_blk, V_W), lambda t: (t, _col_block(COL_GV, V_W))),
            pl.BlockSpec((t_blk, V_W), lambda t: (t, _col_block(COL_GG, V_W))),
            pl.BlockSpec((t_blk, LANE), lambda t: (t, COL_SMALL)),
            pl.BlockSpec((t_blk, t_blk), lambda t: (0, 0)),
            pl.BlockSpec((t_blk, t_blk), lambda t: (0, 0)),
            pl.BlockSpec((None, LANE, QK_W), lambda t: (layer, 0, 0)),
            pl.BlockSpec((None, 1, QK_W), lambda t: (layer, 0, 0)),
            pl.BlockSpec((None, 1, V_W), lambda t: (layer, 0, 0)),
        ],
        out_specs=pl.BlockSpec((t_blk, V_W), lambda t: (t, 0)),
        out_shape=jax.ShapeDtypeStruct((s, V_W), BF16),
        scratch_shapes=[pltpu.VMEM((N_HEADS, V_DIM, QK_DIM), F32)],
        compiler_params=_cparams(("arbitrary",)),
        name="gla",
    )(z, z, z, z, z, tri, mask, wa_pad, ba, gn)


def _mlstm_kernel(qk_ref, v_ref, mo_ref, sm_ref, tri_ref, mask_ref, rep_ref, cw_ref, cb_ref, gb_ref, mn_ref,
                  o_ref, u_ref, c_ref, n_ref, m_ref, *, t_blk):
    @pl.when(pl.program_id(0) == 0)
    def _():
        u_ref[0:SUBLANE, :] = jnp.zeros((SUBLANE, 2 * QK_W), F32)
        c_ref[...] = jnp.zeros_like(c_ref)
        n_ref[...] = jnp.zeros_like(n_ref)
        m_ref[...] = jnp.full(m_ref.shape, -1e30, F32)

    u_ref[SUBLANE:SUBLANE + t_blk, :] = qk_ref[...]
    y = cb_ref[...]
    for j in range(CONV_W):
        off = SUBLANE - (CONV_W - 1) + j
        y = y + cw_ref[j:j + 1, :] * u_ref[off:off + t_blk, :]
    u_ref[0:SUBLANE, :] = u_ref[t_blk:t_blk + SUBLANE, :]
    qk_all = y * _sigmoid(y)

    pre = sm_ref[...] + gb_ref[...]
    f_cum = _chunk_cumsum(_log_sigmoid(pre), tri_ref[...])
    lane = lax.broadcasted_iota(jnp.int32, (t_blk, LANE), 1)
    is_f = jnp.logical_and(lane >= SM_MF, lane < SM_MF + N_HEADS)
    cols = jnp.where(is_f, f_cum, pre)
    rows_t = cols.T

    allowed = mask_ref[...] > 0.5
    n_chunks = t_blk // CHUNK
    chunk_rows = [slice(c * CHUNK, (c + 1) * CHUNK) for c in range(n_chunks)]
    row_chunk = lax.broadcasted_iota(jnp.int32, (t_blk, LANE), 0) // CHUNK
    lane_chunk = lax.broadcasted_iota(jnp.int32, (1, t_blk), 1) // CHUNK
    rep = _split_dot(cols, rep_ref[...])

    def per_chunk_rep(vals):
        out = jnp.broadcast_to(vals[-1], (t_blk, LANE))
        for c in range(n_chunks - 2, -1, -1):
            out = jnp.where(row_chunk == c, vals[c], out)
        return out

    def lanes2(x):
        return jnp.concatenate([x] * (V_DIM // LANE), axis=1)

    k_scale = QK_DIM ** -0.5
    heads = range(N_HEADS)
    vcs = [slice(h * V_DIM, (h + 1) * V_DIM) for h in heads]
    q = [qk_all[:, h * QK_DIM:(h + 1) * QK_DIM] for h in heads]
    k = [qk_all[:, QK_W + h * QK_DIM:QK_W + (h + 1) * QK_DIM] * k_scale for h in heads]
    qb = [x.astype(BF16) for x in q]
    v = [v_ref[:, vc].astype(BF16) for vc in vcs]
    li_rep = [rep[:, (2 * h) * LANE:(2 * h + 1) * LANE] for h in heads]
    f_rep = [rep[:, (2 * h + 1) * LANE:(2 * h + 2) * LANE] for h in heads]
    lir = [rows_t[SM_MI + h:SM_MI + h + 1, :] for h in heads]
    fr = [rows_t[SM_MF + h:SM_MF + h + 1, :] for h in heads]

    f_last, g, d_c, d_n, lmat, l_max, qk = [], [], [], [], [], [], []
    for h in heads:
        fl = [fr[h][:, (c + 1) * CHUNK - 1:(c + 1) * CHUNK] for c in range(n_chunks)]
        fl_row = fl[-1]
        for c in range(n_chunks - 2, -1, -1):
            fl_row = jnp.where(lane_chunk == c, fl[c], fl_row)
        g_row = fl_row - fr[h] + lir[h]
        gh = [jnp.max(jnp.where(lane_chunk == c, g_row, -jnp.inf), axis=-1, keepdims=True)
              for c in range(n_chunks)]
        kw = k[h] * jnp.exp(per_chunk_rep(fl) - f_rep[h] + li_rep[h] - per_chunk_rep(gh))
        kwb = kw.astype(BF16)
        f_last.append(fl)
        g.append(gh)
        d_c.append([_dot_tn(kwb[r], v[h][r]) for r in chunk_rows])
        d_n.append([jnp.sum(kw[r], axis=0, keepdims=True) for r in chunk_rows])
        lm = jnp.where(allowed, lanes2(f_rep[h]) - fr[h] + lir[h], -jnp.inf)
        lmat.append(lm)
        l_max.append(jnp.max(lm, axis=-1, keepdims=True))
        qk.append(_dot_nt(qb[h], k[h].astype(BF16)))

    m_prev = [m_ref[h][:, 0:1] for h in heads]
    c_prev = [c_ref[h] for h in heads]
    n_prev = [n_ref[h] for h in heads]
    m_in = [[] for _ in heads]
    inter = [[] for _ in heads]
    qn = [[] for _ in heads]
    for c, r in enumerate(chunk_rows):
        for h in heads:
            m_in[h].append(m_prev[h])
            inter[h].append(jnp.dot(qb[h][r], c_prev[h].astype(BF16),
                                    preferred_element_type=F32))
            qn[h].append(jnp.sum(q[h][r] * n_prev[h], axis=-1, keepdims=True))
            m_new = jnp.maximum(f_last[h][c] + m_prev[h], g[h][c])
            a = jnp.exp(f_last[h][c] + m_prev[h] - m_new)
            bb = jnp.exp(g[h][c] - m_new)
            c_prev[h] = a * c_prev[h] + bb * d_c[h][c]
            n_prev[h] = a * n_prev[h] + bb * d_n[h][c]
            m_prev[h] = m_new
    for h in heads:
        c_ref[h] = c_prev[h]
        n_ref[h] = n_prev[h]
        m_ref[h] = jnp.broadcast_to(m_prev[h], (1, LANE))

    for h in heads:
        m_inter = f_rep[h] + per_chunk_rep(m_in[h])
        m = jnp.maximum(l_max[h], m_inter)
        s_mat = qk[h] * jnp.exp(lmat[h] - lanes2(m))
        w_inter = jnp.exp(m_inter - m)
        num = jnp.dot(s_mat.astype(BF16), v[h], preferred_element_type=F32)
        num += lanes2(w_inter) * jnp.concatenate(inter[h], axis=0)
        den = jnp.sum(s_mat, axis=-1, keepdims=True)
        den = den + w_inter * jnp.concatenate(qn[h], axis=0)
        hid = num / lanes2(jnp.maximum(jnp.abs(den), jnp.exp(-m)))
        o = _sigmoid(mo_ref[:, vcs[h]]) * hid
        o_ref[:, vcs[h]] = _head_rmsnorm(o, mn_ref[:, vcs[h]]).astype(o_ref.dtype)


def mlstm_branch(z, tri, mask, rep_sel, conv_w, conv_b, gate_bias, mn, layer, *, t_blk=256):
    s = z.shape[0]
    return pl.pallas_call(
        functools.partial(_mlstm_kernel, t_blk=t_blk),
        grid=(s // t_blk,),
        in_specs=[
            pl.BlockSpec((t_blk, 2 * QK_W), lambda t: (t, _col_block(COL_MQK, 2 * QK_W))),
            pl.BlockSpec((t_blk, V_W), lambda t: (t, _col_block(COL_MV, V_W))),
            pl.BlockSpec((t_blk, V_W), lambda t: (t, _col_block(COL_MO, V_W))),
            pl.BlockSpec((t_blk, LANE), lambda t: (t, COL_SMALL)),
            pl.BlockSpec((t_blk, t_blk), lambda t: (0, 0)),
            pl.BlockSpec((t_blk, t_blk), lambda t: (0, 0)),
            pl.BlockSpec((LANE, 2 * N_HEADS * LANE), lambda t: (0, 0)),
            pl.BlockSpec((None, CONV_W, 2 * QK_W), lambda t: (layer, 0, 0)),
            pl.BlockSpec((None, 1, 2 * QK_W), lambda t: (layer, 0, 0)),
            pl.BlockSpec((None, 1, LANE), lambda t: (layer, 0, 0)),
            pl.BlockSpec((None, 1, V_W), lambda t: (layer, 0, 0)),
        ],
        out_specs=pl.BlockSpec((t_blk, V_W), lambda t: (t, 0)),
        out_shape=jax.ShapeDtypeStruct((s, V_W), BF16),
        scratch_shapes=[pltpu.VMEM((SUBLANE + t_blk, 2 * QK_W), F32),
                        pltpu.VMEM((N_HEADS, QK_DIM, V_DIM), F32),
                        pltpu.VMEM((N_HEADS, 1, QK_DIM), F32),
                        pltpu.VMEM((N_HEADS, 1, LANE), F32)],
        compiler_params=_cparams(("arbitrary",)),
        name="mlstm",
    )(z, z, z, z, tri, mask, rep_sel, conv_w, conv_b, gate_bias, mn)


def _memattn_kernel(q_ref, km_ref, vm_ref, o_ref):
    scale = V_DIM ** -0.5
    for h in range(N_HEADS):
        cols = slice(h * V_DIM, (h + 1) * V_DIM)
        q = q_ref[:, cols].astype(BF16)
        s = _dot_nt(q, km_ref[:, cols]) * scale
        e = jnp.exp(s - jnp.max(s, axis=-1, keepdims=True))
        p = e / jnp.sum(e, axis=-1, keepdims=True)
        o = jnp.dot(p.astype(BF16), vm_ref[:, cols], preferred_element_type=F32)
        o_ref[:, cols] = o.astype(o_ref.dtype)


def memattn_branch(z, km, vm, *, tm=512):
    s = z.shape[0]
    tm = min(tm, s)
    return pl.pallas_call(
        _memattn_kernel,
        grid=(s // tm,),
        in_specs=[pl.BlockSpec((tm, V_W), lambda i: (i, _col_block(COL_XQ, V_W))),
                  pl.BlockSpec((MEM_LEN, V_W), lambda i: (0, 0)),
                  pl.BlockSpec((MEM_LEN, V_W), lambda i: (0, 0))],
        out_specs=pl.BlockSpec((tm, V_W), lambda i: (i, 0)),
        out_shape=jax.ShapeDtypeStruct((s, V_W), BF16),
        compiler_params=_cparams(("parallel",)),
        name="memattn",
    )(z, km, vm)


def _merge_kernel(o1_ref, o2_ref, o3_ref, gl_ref, wb_ref, wg_ref, bg_ref, y_ref,
                  wbb_ref, wgb_ref):
    @pl.when(pl.program_id(1) == 0)
    def _():
        wbb_ref[...] = wb_ref[...].astype(BF16)
        wgb_ref[...] = wg_ref[...].astype(BF16)

    gl = gl_ref[...].astype(BF16)
    y = None
    for j, o_ref in enumerate((o1_ref, o2_ref, o3_ref)):
        gate = _sigmoid(jnp.dot(gl, wgb_ref[j], preferred_element_type=F32)
                        + bg_ref[j:j + 1, :])
        term = gate * jnp.dot(o_ref[...], wbb_ref[j], preferred_element_type=F32)
        y = term if y is None else y + term
    y_ref[...] = y.astype(y_ref.dtype)


def merge_branches(o1, o2, o3, z, wb, wg, bg, layer, *, tm=1024, tn=512):
    s, w = o1.shape
    d = wb.shape[-1]
    tm = min(tm, s)
    o_spec = pl.BlockSpec((tm, w), lambda j, i: (i, 0))
    return pl.pallas_call(
        _merge_kernel,
        grid=(d // tn, s // tm),
        in_specs=[o_spec, o_spec, o_spec,
                  pl.BlockSpec((tm, GATE_RANK), lambda j, i: (i, _col_block(COL_GATE, GATE_RANK))),
                  pl.BlockSpec((None, N_BRANCH, w, tn), lambda j, i: (layer, 0, 0, j)),
                  pl.BlockSpec((None, N_BRANCH, GATE_RANK, tn), lambda j, i: (layer, 0, 0, j)),
                  pl.BlockSpec((None, N_BRANCH, tn), lambda j, i: (layer, 0, j))],
        out_specs=pl.BlockSpec((tm, tn), lambda j, i: (i, j)),
        out_shape=jax.ShapeDtypeStruct((s, d), BF16),
        scratch_shapes=[pltpu.VMEM((N_BRANCH, w, tn), BF16),
                        pltpu.VMEM((N_BRANCH, GATE_RANK, tn), BF16)],
        compiler_params=_cparams(("parallel", "arbitrary")),
        name="merge",
    )(o1, o2, o3, z, wb, wg, bg)


_IN_GROUPS = ("gq", "gk", "gv", "gg", "ga", "mqk", "mv", "mo", "mi", "mf", "xq", "gate")
_IN_SIZES = (QK_W, QK_W, V_W, V_W, GLA_RANK, 2 * QK_W, V_W, V_W, N_HEADS, N_HEADS, V_W, GATE_RANK)
_IN_SRC = {name: sum(_IN_SIZES[:i]) for i, name in enumerate(_IN_GROUPS)}
_IN_WIDTH = dict(zip(_IN_GROUPS, _IN_SIZES))
_IN_DST = {"gv": COL_GV, "gg": COL_GG, "mv": COL_MV, "mo": COL_MO, "xq": COL_XQ,
           "mqk": COL_MQK, "gq": COL_GQ, "gk": COL_GK, "gate": COL_GATE}
_IN_SMALL = ("ga", "mi", "mf")
assert (SM_GA, SM_MI, SM_MF) == (0, GLA_RANK, GLA_RANK + N_HEADS)


def _permute_w_kernel(w_ref, o_ref):
    rows = w_ref.shape[0]
    for name, col in _IN_DST.items():
        src, width = _IN_SRC[name], _IN_WIDTH[name]
        o_ref[:, col * LANE:col * LANE + width] = w_ref[:, src:src + width].astype(o_ref.dtype)
    small = [w_ref[:, _IN_SRC[n]:_IN_SRC[n] + _IN_WIDTH[n]] for n in _IN_SMALL]
    used = sum(_IN_WIDTH[n] for n in _IN_SMALL)
    small.append(jnp.zeros((rows, (IN_COLS - COL_SMALL) * LANE - used), F32))
    o_ref[:, COL_SMALL * LANE:] = jnp.concatenate(small, axis=1).astype(o_ref.dtype)


def permute_w_in(w_in, *, tk=256):
    depth, d, n_src = w_in.shape
    n_dst = IN_COLS * LANE
    return pl.pallas_call(
        _permute_w_kernel,
        grid=(depth, d // tk),
        in_specs=[pl.BlockSpec((None, tk, n_src), lambda l, i: (l, i, 0))],
        out_specs=pl.BlockSpec((None, tk, n_dst), lambda l, i: (l, i, 0)),
        out_shape=jax.ShapeDtypeStruct((depth, d, n_dst), BF16),
        compiler_params=_cparams(("parallel", "parallel")),
        name="permute_w_in",
    )(w_in)


def kernel(x, mem, norm_mix, w_in, w_gla_a, b_gla_a, gla_norm, conv_w, conv_b, b_ml_i, b_ml_f, ml_norm, mem_norm, w_mem_k, w_mem_v, w_branch, w_gate, b_gate, w_out, norm_ffn, w_ff1, w_ff2, final_norm):
    depth = w_in.shape[0]
    _, s, d = x.shape
    xs = x.reshape(s, d)
    t_blk = min(256, s)

    w_in_p = permute_w_in(w_in)
    wa_pad = jnp.zeros((depth, LANE, QK_W), F32)
    wa_pad = wa_pad.at[:, SM_GA:SM_GA + GLA_RANK, :].set(w_gla_a).astype(BF16)
    gate_bias = jnp.zeros((depth, 1, LANE), F32)
    gate_bias = gate_bias.at[:, 0, SM_MI:SM_MI + N_HEADS].set(b_ml_i)
    gate_bias = gate_bias.at[:, 0, SM_MF:SM_MF + N_HEADS].set(b_ml_f)
    w_out_b = w_out.astype(BF16)
    w_ff2_b = w_ff2.astype(BF16)
    tri = chunk_tri(t_blk)
    tri_f = tri.astype(F32)
    rep_sel = gate_replicator()
    b_gla_a3 = b_gla_a.reshape(depth, 1, QK_W)
    gla_norm3 = gla_norm.reshape(depth, 1, V_W)
    conv_b3 = conv_b.reshape(depth, 1, 2 * QK_W)
    ml_norm3 = ml_norm.reshape(depth, 1, V_W)

    memn = rmsnorm(mem.reshape(MEM_LEN, d), mem_norm, BF16, tm=MEM_LEN)

    for l in range(depth):
        h = rmsnorm(xs, norm_mix[l], BF16)
        z = matmul_rows(h, w_in_p, l, tm=1024, tn=768, out_dtype=F32, name="in_proj")
        km = matmul_cols(memn, w_mem_k, l, tm=MEM_LEN, tn=256, out_dtype=BF16, name="mem_k")
        vm = matmul_cols(memn, w_mem_v, l, tm=MEM_LEN, tn=256, out_dtype=BF16, name="mem_v")
        o_gla = gla_branch(z, tri, tri_f, wa_pad, b_gla_a3, gla_norm3, l, t_blk=t_blk)
        o_ml = mlstm_branch(z, tri, tri_f, rep_sel, conv_w, conv_b3, gate_bias, ml_norm3, l, t_blk=t_blk)
        o_mem = memattn_branch(z, km, vm)
        y = merge_branches(o_gla, o_ml, o_mem, z, w_branch, w_gate, b_gate, l)
        xs = matmul_acc(y, w_out_b, l, xs, tm=1024, tn=1024, tk=d, name="out_proj")
        h = rmsnorm(xs, norm_ffn[l], BF16)
        hid = matmul_cols(h, w_ff1, l, tm=1024, tn=512, out_dtype=BF16, epilogue="relu2",
                          name="ffn_up")
        xs = matmul_acc(hid, w_ff2_b, l, xs, tm=1024, tn=1024, tk=4096, name="ffn_down")
    return rmsnorm(xs, final_norm, F32).reshape(x.shape)
```

```python
import functools

import jax
import jax.numpy as jnp
from jax import lax
from jax.experimental import pallas as pl
from jax.experimental.pallas import tpu as pltpu

F32 = jnp.float32
BF16 = jnp.bfloat16

EPS = 1e-6
CHUNK = 64
N_HEADS = 4
QK_DIM = 128
V_DIM = 256
QK_W = N_HEADS * QK_DIM
V_W = N_HEADS * V_DIM
GLA_RANK = 16
GLA_TAU = 16.0
CONV_W = 4
MEM_LEN = 256
GATE_RANK = 256
N_BRANCH = 3

LANE = 128
SUBLANE = 8
VMEM_LIMIT = 56 * 1024 * 1024

COL_GV, COL_GG, COL_MV, COL_MO, COL_XQ = 0, 8, 16, 24, 32
COL_MQK, COL_GQ, COL_GK = 40, 48, 52
COL_GATE, COL_SMALL, IN_COLS = 56, 58, 60
SM_GA, SM_MI, SM_MF = 0, 16, 20


def _cparams(sem):
    return pltpu.CompilerParams(dimension_semantics=sem, vmem_limit_bytes=VMEM_LIMIT)


def _col_block(col, width):
    return col * LANE // width


def _rmsnorm_kernel(x_ref, g_ref, o_ref):
    x = x_ref[...]
    ms = jnp.mean(x * x, axis=-1, keepdims=True)
    o_ref[...] = ((x * lax.rsqrt(ms + EPS)) * g_ref[...]).astype(o_ref.dtype)


def rmsnorm(x, g, out_dtype, tm=256):
    m, d = x.shape
    tm = min(tm, m)
    return pl.pallas_call(
        _rmsnorm_kernel,
        grid=(m // tm,),
        in_specs=[pl.BlockSpec((tm, d), lambda i: (i, 0)),
                  pl.BlockSpec((1, d), lambda i: (0, 0))],
        out_specs=pl.BlockSpec((tm, d), lambda i: (i, 0)),
        out_shape=jax.ShapeDtypeStruct((m, d), out_dtype),
        compiler_params=_cparams(("parallel",)),
        name="rmsnorm",
    )(x, g.reshape(1, d))


def _apply_epilogue(acc, epilogue):
    if epilogue == "relu2":
        return jnp.square(jnp.maximum(acc, 0.0))
    return acc


def _side_cast_specs(srcs, layer, n_steps, step_of):
    in_specs, out_specs, out_shapes = [], [], []
    for w in srcs:
        r, c = w.shape[-2:]
        rows = r // n_steps
        in_specs.append(pl.BlockSpec((None, rows, c), lambda *g: (layer, step_of(*g), 0)))
        out_specs.append(pl.BlockSpec((rows, c), lambda *g: (step_of(*g), 0)))
        out_shapes.append(jax.ShapeDtypeStruct((r, c), BF16))
    return in_specs, out_specs, out_shapes


def _side_cast(side_in, side_out):
    for src_ref, dst_ref in zip(side_in, side_out):
        dst_ref[...] = src_ref[...].astype(dst_ref.dtype)


def _weight_spec(b, layer, block, index):
    if b.ndim == 2:
        return pl.BlockSpec(block, index)
    return pl.BlockSpec((None,) + block, lambda *g: (layer,) + index(*g))


def _mm_rows_kernel(*refs, transposed_b, epilogue, n_side):
    a_ref, b_ref = refs[:2]
    side_in = refs[2:2 + n_side]
    o_ref = refs[2 + n_side]
    side_out = refs[3 + n_side:3 + 2 * n_side]
    dims = (((1,), (1 if transposed_b else 0,)), ((), ()))
    acc = lax.dot_general(a_ref[...], b_ref[...], dims, preferred_element_type=F32)
    o_ref[...] = _apply_epilogue(acc, epilogue).astype(o_ref.dtype)
    _side_cast(side_in, side_out)


def matmul_rows(a, b, layer, *, tm, tn, out_dtype, name, epilogue=None, transposed_b=False,
                side_cast=()):
    m, kdim = a.shape
    n = b.shape[-2] if transposed_b else b.shape[-1]
    tm = min(tm, m)
    grid = (m // tm, n // tn)
    b_spec = (_weight_spec(b, layer, (tn, kdim), lambda i, j: (j, 0)) if transposed_b
              else _weight_spec(b, layer, (kdim, tn), lambda i, j: (0, j)))
    s_in, s_out, s_shapes = _side_cast_specs(side_cast, layer, grid[0] * grid[1],
                                             lambda i, j: i * grid[1] + j)
    return pl.pallas_call(
        functools.partial(_mm_rows_kernel, transposed_b=transposed_b, epilogue=epilogue,
                          n_side=len(side_cast)),
        grid=grid,
        in_specs=[pl.BlockSpec((tm, kdim), lambda i, j: (i, 0)), b_spec] + s_in,
        out_specs=[pl.BlockSpec((tm, tn), lambda i, j: (i, j))] + s_out,
        out_shape=[jax.ShapeDtypeStruct((m, n), out_dtype)] + s_shapes,
        compiler_params=_cparams(("parallel", "parallel")),
        name=name,
    )(a, b, *side_cast)


def _mm_cols_kernel(a_ref, b_ref, o_ref, wb_ref, *, epilogue):
    @pl.when(pl.program_id(1) == 0)
    def _():
        wb_ref[...] = b_ref[...].astype(BF16)

    acc = jnp.dot(a_ref[...].astype(BF16), wb_ref[...], preferred_element_type=F32)
    o_ref[...] = _apply_epilogue(acc, epilogue).astype(o_ref.dtype)


def matmul_cols(a, b, layer, *, tm, tn, out_dtype, epilogue=None, name):
    m, kdim = a.shape
    n = b.shape[-1]
    tm = min(tm, m)
    return pl.pallas_call(
        functools.partial(_mm_cols_kernel, epilogue=epilogue),
        grid=(n // tn, m // tm),
        in_specs=[pl.BlockSpec((tm, kdim), lambda j, i: (i, 0)),
                  pl.BlockSpec((None, kdim, tn), lambda j, i: (layer, 0, j))],
        out_specs=pl.BlockSpec((tm, tn), lambda j, i: (i, j)),
        out_shape=jax.ShapeDtypeStruct((m, n), out_dtype),
        scratch_shapes=[pltpu.VMEM((kdim, tn), BF16)],
        compiler_params=_cparams(("parallel", "arbitrary")),
        name=name,
    )(a, b)


def _mm_acc_kernel(a_ref, b_ref, r_ref, o_ref):
    @pl.when(pl.program_id(2) == 0)
    def _():
        o_ref[...] = r_ref[...]

    o_ref[...] = jnp.dot(a_ref[...], b_ref[...], preferred_element_type=F32) + o_ref[...]


def matmul_acc(a, b, layer, residual, *, tm, tn, tk, name):
    m, kdim = a.shape
    n = b.shape[-1]
    tm = min(tm, m)
    return pl.pallas_call(
        _mm_acc_kernel,
        grid=(m // tm, n // tn, kdim // tk),
        in_specs=[pl.BlockSpec((tm, tk), lambda i, j, k: (i, k)),
                  _weight_spec(b, layer, (tk, tn), lambda i, j, k: (k, j)),
                  pl.BlockSpec((tm, tn), lambda i, j, k: (i, j))],
        out_specs=pl.BlockSpec((tm, tn), lambda i, j, k: (i, j)),
        out_shape=jax.ShapeDtypeStruct((m, n), F32),
        compiler_params=_cparams(("parallel", "parallel", "arbitrary")),
        name=name,
    )(a, b, residual)


def _log_sigmoid(x):
    return jnp.minimum(x, 0.0) - jnp.log1p(jnp.exp(-jnp.abs(x)))


def _sigmoid(x):
    return 1.0 / (1.0 + jnp.exp(-x))


def chunk_tri(t):
    r = lax.broadcasted_iota(jnp.int32, (t, t), 0)
    c = lax.broadcasted_iota(jnp.int32, (t, t), 1)
    same = (r // CHUNK) == (c // CHUNK)
    return jnp.logical_and(same, c <= r).astype(BF16)


def _chunk_cumsum(x, tri):
    hi = x.astype(BF16)
    r1 = x - hi.astype(F32)
    mid = r1.astype(BF16)
    lo = (r1 - mid.astype(F32)).astype(BF16)
    out = jnp.dot(tri, hi, preferred_element_type=F32)
    out += jnp.dot(tri, mid, preferred_element_type=F32)
    out += jnp.dot(tri, lo, preferred_element_type=F32)
    return out


def _split_dot(x, sel):
    hi = x.astype(BF16)
    r1 = x - hi.astype(F32)
    mid = r1.astype(BF16)
    lo = (r1 - mid.astype(F32)).astype(BF16)
    out = jnp.dot(hi, sel, preferred_element_type=F32)
    out += jnp.dot(mid, sel, preferred_element_type=F32)
    out += jnp.dot(lo, sel, preferred_element_type=F32)
    return out


def gate_replicator():
    src = lax.broadcasted_iota(jnp.int32, (LANE, 2 * N_HEADS * LANE), 0)
    grp = lax.broadcasted_iota(jnp.int32, (LANE, 2 * N_HEADS * LANE), 1) // LANE
    want = jnp.where(grp % 2 == 0, SM_MI, SM_MF) + grp // 2
    return (src == want).astype(BF16)


def _dot_nt(a, b):
    return lax.dot_general(a, b, (((1,), (1,)), ((), ())), preferred_element_type=F32)


def _dot_tn(a, b):
    return lax.dot_general(a, b, (((0,), (0,)), ((), ())), preferred_element_type=F32)


def _head_rmsnorm(o, g):
    ms = jnp.mean(o * o, axis=-1, keepdims=True)
    return (o * lax.rsqrt(ms + EPS)) * g


def _gla_kernel(q_ref, k_ref, v_ref, gg_ref, sm_ref, tri_ref, mask_ref, wa_ref, ba_ref, gn_ref,
                o_ref, st_ref, *, t_blk):
    @pl.when(pl.program_id(0) == 0)
    def _():
        st_ref[...] = jnp.zeros_like(st_ref)

    pre = jnp.dot(sm_ref[...].astype(BF16), wa_ref[...],
                  preferred_element_type=F32) + ba_ref[...]
    log_a = _log_sigmoid(pre) / GLA_TAU
    b_all = _chunk_cumsum(log_a, tri_ref[...])
    allowed = mask_ref[...] > 0.5
    scale = QK_DIM ** -0.5
    n_chunks = t_blk // CHUNK
    chunk_rows = [slice(c * CHUNK, (c + 1) * CHUNK) for c in range(n_chunks)]
    heads = range(N_HEADS)
    vcs = [slice(h * V_DIM, (h + 1) * V_DIM) for h in heads]

    q_d, o_intra, d_st, decay = [], [], [], []
    for h in heads:
        kc = slice(h * QK_DIM, (h + 1) * QK_DIM)
        b = b_all[:, kc]
        b_last = [b[r][CHUNK - 1:CHUNK] for r in chunk_rows]
        b_last_rows = jnp.concatenate(
            [jnp.broadcast_to(bl, (CHUNK, QK_DIM)) for bl in b_last], axis=0)
        k = k_ref[:, kc]
        v = v_ref[:, vcs[h]].astype(BF16)
        qd = ((q_ref[:, kc] * scale) * jnp.exp(b)).astype(BF16)
        k_d = (k * jnp.exp(-b)).astype(BF16)
        k_s = (k * jnp.exp(b_last_rows - b)).astype(BF16)
        att = jnp.where(allowed, _dot_nt(qd, k_d), 0.0)
        q_d.append(qd)
        o_intra.append(jnp.dot(att.astype(BF16), v, preferred_element_type=F32))
        d_st.append([_dot_tn(v[r], k_s[r]) for r in chunk_rows])
        decay.append([jnp.exp(bl) for bl in b_last])

    states = [st_ref[h] for h in heads]
    inter = [[] for _ in heads]
    for c, r in enumerate(chunk_rows):
        for h in heads:
            inter[h].append(_dot_nt(q_d[h][r], states[h].astype(BF16)))
            states[h] = states[h] * decay[h][c] + d_st[h][c]
    for h in heads:
        st_ref[h] = states[h]

    for h in heads:
        o = o_intra[h] + jnp.concatenate(inter[h], axis=0)
        gg = gg_ref[:, vcs[h]]
        o = _head_rmsnorm(o, gn_ref[:, vcs[h]]) * (gg * _sigmoid(gg))
        o_ref[:, vcs[h]] = o.astype(o_ref.dtype)


def gla_branch(z, tri, mask, wa_pad, ba, gn, layer, *, t_blk=256):
    s = z.shape[0]
    return pl.pallas_call(
        functools.partial(_gla_kernel, t_blk=t_blk),
        grid=(s // t_blk,),
        in_specs=[
            pl.BlockSpec((t_blk, QK_W), lambda t: (t, _col_block(COL_GQ, QK_W))),
            pl.BlockSpec((t_blk, QK_W), lambda t: (t, _col_block(COL_GK, QK_W))),
            pl.BlockSpec((t_blk, V_W), lambda t: (t, _col_block(COL_GV, V_W))),
            pl.BlockSpec((t_blk, V_W), lambda t: (t, _col_block(COL_GG, V_W))),
            pl.BlockSpec((t_blk, LANE), lambda t: (t, COL_SMALL)),
            pl.BlockSpec((t_blk, t_blk), lambda t: (0, 0)),
            pl.BlockSpec((t_blk, t_blk), lambda t: (0, 0)),
            pl.BlockSpec((None, LANE, QK_W), lambda t: (layer, 0, 0)),
            pl.BlockSpec((None, 1, QK_W), lambda t: (layer, 0, 0)),
            pl.BlockSpec((None, 1, V_W), lambda t: (layer, 0, 0)),
        ],
        out_specs=pl.BlockSpec((t_blk, V_W), lambda t: (t, 0)),
        out_shape=jax.ShapeDtypeStruct((s, V_W), BF16),
        scratch_shapes=[pltpu.VMEM((N_HEADS, V_DIM, QK_DIM), F32)],
        compiler_params=_cparams(("arbitrary",)),
        name="gla",
    )(z, z, z, z, z, tri, mask, wa_pad, ba, gn)


def _mlstm_kernel(qk_ref, v_ref, mo_ref, sm_ref, tri_ref, mask_ref, rep_ref, cw_ref, cb_ref, gb_ref, mn_ref,
                  o_ref, u_ref, c_ref, n_ref, m_ref, *, t_blk):
    @pl.when(pl.program_id(0) == 0)
    def _():
        u_ref[0:SUBLANE, :] = jnp.zeros((SUBLANE, 2 * QK_W), F32)
        c_ref[...] = jnp.zeros_like(c_ref)
        n_ref[...] = jnp.zeros_like(n_ref)
        m_ref[...] = jnp.full(m_ref.shape, -1e30, F32)

    u_ref[SUBLANE:SUBLANE + t_blk, :] = qk_ref[...]
    y = cb_ref[...]
    for j in range(CONV_W):
        off = SUBLANE - (CONV_W - 1) + j
        y = y + cw_ref[j:j + 1, :] * u_ref[off:off + t_blk, :]
    u_ref[0:SUBLANE, :] = u_ref[t_blk:t_blk + SUBLANE, :]
    qk_all = y * _sigmoid(y)

    pre = sm_ref[...] + gb_ref[...]
    f_cum = _chunk_cumsum(_log_sigmoid(pre), tri_ref[...])
    lane = lax.broadcasted_iota(jnp.int32, (t_blk, LANE), 1)
    is_f = jnp.logical_and(lane >= SM_MF, lane < SM_MF + N_HEADS)
    cols = jnp.where(is_f, f_cum, pre)
    rows_t = cols.T

    allowed = mask_ref[...] > 0.5
    n_chunks = t_blk // CHUNK
    chunk_rows = [slice(c * CHUNK, (c + 1) * CHUNK) for c in range(n_chunks)]
    row_chunk = lax.broadcasted_iota(jnp.int32, (t_blk, LANE), 0) // CHUNK
    lane_chunk = lax.broadcasted_iota(jnp.int32, (1, t_blk), 1) // CHUNK
    rep = _split_dot(cols, rep_ref[...])

    def per_chunk_rep(vals):
        out = jnp.broadcast_to(vals[-1], (t_blk, LANE))
        for c in range(n_chunks - 2, -1, -1):
            out = jnp.where(row_chunk == c, vals[c], out)
        return out

    def lanes2(x):
        return jnp.concatenate([x] * (V_DIM // LANE), axis=1)

    k_scale = QK_DIM ** -0.5
    heads = range(N_HEADS)
    vcs = [slice(h * V_DIM, (h + 1) * V_DIM) for h in heads]
    q = [qk_all[:, h * QK_DIM:(h + 1) * QK_DIM] for h in heads]
    k = [qk_all[:, QK_W + h * QK_DIM:QK_W + (h + 1) * QK_DIM] * k_scale for h in heads]
    qb = [x.astype(BF16) for x in q]
    v = [v_ref[:, vc].astype(BF16) for vc in vcs]
    li_rep = [rep[:, (2 * h) * LANE:(2 * h + 1) * LANE] for h in heads]
    f_rep = [rep[:, (2 * h + 1) * LANE:(2 * h + 2) * LANE] for h in heads]
    lir = [rows_t[SM_MI + h:SM_MI + h + 1, :] for h in heads]
    fr = [rows_t[SM_MF + h:SM_MF + h + 1, :] for h in heads]

    f_last, g, d_c, d_n, lmat, l_max, qk = [], [], [], [], [], [], []
    for h in heads:
        fl = [fr[h][:, (c + 1) * CHUNK - 1:(c + 1) * CHUNK] for c in range(n_chunks)]
        fl_row = fl[-1]
        for c in range(n_chunks - 2, -1, -1):
            fl_row = jnp.where(lane_chunk == c, fl[c], fl_row)
        g_row = fl_row - fr[h] + lir[h]
        gh = [jnp.max(jnp.where(lane_chunk == c, g_row, -jnp.inf), axis=-1, keepdims=True)
              for c in range(n_chunks)]
        kw = k[h] * jnp.exp(per_chunk_rep(fl) - f_rep[h] + li_rep[h] - per_chunk_rep(gh))
        kwb = kw.astype(BF16)
        f_last.append(fl)
        g.append(gh)
        d_c.append([_dot_tn(kwb[r], v[h][r]) for r in chunk_rows])
        d_n.append([jnp.sum(kw[r], axis=0, keepdims=True) for r in chunk_rows])
        lm = jnp.where(allowed, lanes2(f_rep[h]) - fr[h] + lir[h], -jnp.inf)
        lmat.append(lm)
        l_max.append(jnp.max(lm, axis=-1, keepdims=True))
        qk.append(_dot_nt(qb[h], k[h].astype(BF16)))

    m_prev = [m_ref[h][:, 0:1] for h in heads]
    c_prev = [c_ref[h] for h in heads]
    n_prev = [n_ref[h] for h in heads]
    m_in = [[] for _ in heads]
    inter = [[] for _ in heads]
    qn = [[] for _ in heads]
    for c, r in enumerate(chunk_rows):
        for h in heads:
            m_in[h].append(m_prev[h])
            inter[h].append(jnp.dot(qb[h][r], c_prev[h].astype(BF16),
                                    preferred_element_type=F32))
            qn[h].append(jnp.sum(q[h][r] * n_prev[h], axis=-1, keepdims=True))
            m_new = jnp.maximum(f_last[h][c] + m_prev[h], g[h][c])
            a = jnp.exp(f_last[h][c] + m_prev[h] - m_new)
            bb = jnp.exp(g[h][c] - m_new)
            c_prev[h] = a * c_prev[h] + bb * d_c[h][c]
            n_prev[h] = a * n_prev[h] + bb * d_n[h][c]
            m_prev[h] = m_new
    for h in heads:
        c_ref[h] = c_prev[h]
        n_ref[h] = n_prev[h]
        m_ref[h] = jnp.broadcast_to(m_prev[h], (1, LANE))

    for h in heads:
        m_inter = f_rep[h] + per_chunk_rep(m_in[h])
        m = jnp.maximum(l_max[h], m_inter)
        s_mat = qk[h] * jnp.exp(lmat[h] - lanes2(m))
        w_inter = jnp.exp(m_inter - m)
        num = jnp.dot(s_mat.astype(BF16), v[h], preferred_element_type=F32)
        num += lanes2(w_inter) * jnp.concatenate(inter[h], axis=0)
        den = jnp.sum(s_mat, axis=-1, keepdims=True)
        den = den + w_inter * jnp.concatenate(qn[h], axis=0)
        hid = num / lanes2(jnp.maximum(jnp.abs(den), jnp.exp(-m)))
        o = _sigmoid(mo_ref[:, vcs[h]]) * hid
        o_ref[:, vcs[h]] = _head_rmsnorm(o, mn_ref[:, vcs[h]]).astype(o_ref.dtype)


def mlstm_branch(z, tri, mask, rep_sel, conv_w, conv_b, gate_bias, mn, layer, *, t_blk=256):
    s = z.shape[0]
    return pl.pallas_call(
        functools.partial(_mlstm_kernel, t_blk=t_blk),
        grid=(s // t_blk,),
        in_specs=[
            pl.BlockSpec((t_blk, 2 * QK_W), lambda t: (t, _col_block(COL_MQK, 2 * QK_W))),
            pl.BlockSpec((t_blk, V_W), lambda t: (t, _col_block(COL_MV, V_W))),
            pl.BlockSpec((t_blk, V_W), lambda t: (t, _col_block(COL_MO, V_W))),
            pl.BlockSpec((t_blk, LANE), lambda t: (t, COL_SMALL)),
            pl.BlockSpec((t_blk, t_blk), lambda t: (0, 0)),
            pl.BlockSpec((t_blk, t_blk), lambda t: (0, 0)),
            pl.BlockSpec((LANE, 2 * N_HEADS * LANE), lambda t: (0, 0)),
            pl.BlockSpec((None, CONV_W, 2 * QK_W), lambda t: (layer, 0, 0)),
            pl.BlockSpec((None, 1, 2 * QK_W), lambda t: (layer, 0, 0)),
            pl.BlockSpec((None, 1, LANE), lambda t: (layer, 0, 0)),
            pl.BlockSpec((None, 1, V_W), lambda t: (layer, 0, 0)),
        ],
        out_specs=pl.BlockSpec((t_blk, V_W), lambda t: (t, 0)),
        out_shape=jax.ShapeDtypeStruct((s, V_W), BF16),
        scratch_shapes=[pltpu.VMEM((SUBLANE + t_blk, 2 * QK_W), F32),
                        pltpu.VMEM((N_HEADS, QK_DIM, V_DIM), F32),
                        pltpu.VMEM((N_HEADS, 1, QK_DIM), F32),
                        pltpu.VMEM((N_HEADS, 1, LANE), F32)],
        compiler_params=_cparams(("arbitrary",)),
        name="mlstm",
    )(z, z, z, z, tri, mask, rep_sel, conv_w, conv_b, gate_bias, mn)


def _memattn_kernel(q_ref, km_ref, vm_ref, o_ref):
    scale = V_DIM ** -0.5
    for h in range(N_HEADS):
        cols = slice(h * V_DIM, (h + 1) * V_DIM)
        q = q_ref[:, cols].astype(BF16)
        s = _dot_nt(q, km_ref[:, cols]) * scale
        e = jnp.exp(s - jnp.max(s, axis=-1, keepdims=True))
        p = e / jnp.sum(e, axis=-1, keepdims=True)
        o = jnp.dot(p.astype(BF16), vm_ref[:, cols], preferred_element_type=F32)
        o_ref[:, cols] = o.astype(o_ref.dtype)


def memattn_branch(z, km, vm, *, tm=512):
    s = z.shape[0]
    tm = min(tm, s)
    return pl.pallas_call(
        _memattn_kernel,
        grid=(s // tm,),
        in_specs=[pl.BlockSpec((tm, V_W), lambda i: (i, _col_block(COL_XQ, V_W))),
                  pl.BlockSpec((MEM_LEN, V_W), lambda i: (0, 0)),
                  pl.BlockSpec((MEM_LEN, V_W), lambda i: (0, 0))],
        out_specs=pl.BlockSpec((tm, V_W), lambda i: (i, 0)),
        out_shape=jax.ShapeDtypeStruct((s, V_W), BF16),
        compiler_params=_cparams(("parallel",)),
        name="memattn",
    )(z, km, vm)


def _merge_kernel(*refs, n_side):
    o1_ref, o2_ref, o3_ref, gl_ref, wb_ref, wg_ref, bg_ref = refs[:7]
    side_in = refs[7:7 + n_side]
    y_ref = refs[7 + n_side]
    side_out = refs[8 + n_side:8 + 2 * n_side]
    wbb_ref, wgb_ref = refs[8 + 2 * n_side:]
    _side_cast(side_in, side_out)

    @pl.when(pl.program_id(1) == 0)
    def _():
        wbb_ref[...] = wb_ref[...].astype(BF16)
        wgb_ref[...] = wg_ref[...].astype(BF16)

    gl = gl_ref[...].astype(BF16)
    y = None
    for j, o_ref in enumerate((o1_ref, o2_ref, o3_ref)):
        gate = _sigmoid(jnp.dot(gl, wgb_ref[j], preferred_element_type=F32)
                        + bg_ref[j:j + 1, :])
        term = gate * jnp.dot(o_ref[...], wbb_ref[j], preferred_element_type=F32)
        y = term if y is None else y + term
    y_ref[...] = y.astype(y_ref.dtype)


def merge_branches(o1, o2, o3, z, wb, wg, bg, layer, *, tm=1024, tn=512, side_cast=()):
    s, w = o1.shape
    d = wb.shape[-1]
    tm = min(tm, s)
    grid = (d // tn, s // tm)
    o_spec = pl.BlockSpec((tm, w), lambda j, i: (i, 0))
    s_in, s_out, s_shapes = _side_cast_specs(side_cast, layer, grid[0] * grid[1],
                                             lambda j, i: j * grid[1] + i)
    return pl.pallas_call(
        functools.partial(_merge_kernel, n_side=len(side_cast)),
        grid=grid,
        in_specs=[o_spec, o_spec, o_spec,
                  pl.BlockSpec((tm, GATE_RANK), lambda j, i: (i, _col_block(COL_GATE, GATE_RANK))),
                  pl.BlockSpec((None, N_BRANCH, w, tn), lambda j, i: (layer, 0, 0, j)),
                  pl.BlockSpec((None, N_BRANCH, GATE_RANK, tn), lambda j, i: (layer, 0, 0, j)),
                  pl.BlockSpec((None, N_BRANCH, tn), lambda j, i: (layer, 0, j))] + s_in,
        out_specs=[pl.BlockSpec((tm, tn), lambda j, i: (i, j))] + s_out,
        out_shape=[jax.ShapeDtypeStruct((s, d), BF16)] + s_shapes,
        scratch_shapes=[pltpu.VMEM((N_BRANCH, w, tn), BF16),
                        pltpu.VMEM((N_BRANCH, GATE_RANK, tn), BF16)],
        compiler_params=_cparams(("parallel", "arbitrary")),
        name="merge",
    )(o1, o2, o3, z, wb, wg, bg, *side_cast)


_IN_GROUPS = ("gq", "gk", "gv", "gg", "ga", "mqk", "mv", "mo", "mi", "mf", "xq", "gate")
_IN_SIZES = (QK_W, QK_W, V_W, V_W, GLA_RANK, 2 * QK_W, V_W, V_W, N_HEADS, N_HEADS, V_W, GATE_RANK)
_IN_SRC = {name: sum(_IN_SIZES[:i]) for i, name in enumerate(_IN_GROUPS)}
_IN_WIDTH = dict(zip(_IN_GROUPS, _IN_SIZES))
_IN_DST = {"gv": COL_GV, "gg": COL_GG, "mv": COL_MV, "mo": COL_MO, "xq": COL_XQ,
           "mqk": COL_MQK, "gq": COL_GQ, "gk": COL_GK, "gate": COL_GATE}
_IN_SMALL = ("ga", "mi", "mf")
assert (SM_GA, SM_MI, SM_MF) == (0, GLA_RANK, GLA_RANK + N_HEADS)


def _permute_wt_kernel(w_ref, o_ref):
    lanes = w_ref.shape[1]
    for name, col in _IN_DST.items():
        src, width = _IN_SRC[name], _IN_WIDTH[name]
        o_ref[col * LANE:col * LANE + width, :] = w_ref[src:src + width, :].astype(o_ref.dtype)
    spans = []
    for n in _IN_SMALL:
        if spans and spans[-1][0] + spans[-1][1] == _IN_SRC[n]:
            spans[-1] = (spans[-1][0], spans[-1][1] + _IN_WIDTH[n])
        else:
            spans.append((_IN_SRC[n], _IN_WIDTH[n]))
    small = [w_ref[src:src + width, :] for src, width in spans]
    used = sum(width for _, width in spans)
    small.append(jnp.zeros(((IN_COLS - COL_SMALL) * LANE - used, lanes), F32))
    o_ref[COL_SMALL * LANE:, :] = jnp.concatenate(small, axis=0).astype(o_ref.dtype)


def permute_w_in_t(w_in_t, *, tc=256):
    depth, n_src, d = w_in_t.shape
    n_dst = IN_COLS * LANE
    return pl.pallas_call(
        _permute_wt_kernel,
        grid=(depth, d // tc),
        in_specs=[pl.BlockSpec((None, n_src, tc), lambda l, i: (l, 0, i))],
        out_specs=pl.BlockSpec((None, n_dst, tc), lambda l, i: (l, 0, i)),
        out_shape=jax.ShapeDtypeStruct((depth, n_dst, d), BF16),
        compiler_params=_cparams(("parallel", "parallel")),
        name="permute_w_in",
    )(w_in_t)


def kernel(x, mem, norm_mix, w_in, w_gla_a, b_gla_a, gla_norm, conv_w, conv_b, b_ml_i, b_ml_f, ml_norm, mem_norm, w_mem_k, w_mem_v, w_branch, w_gate, b_gate, w_out, norm_ffn, w_ff1, w_ff2, final_norm):
    depth = w_in.shape[0]
    _, s, d = x.shape
    xs = x.reshape(s, d)
    t_blk = min(256, s)

    w_in_pt = permute_w_in_t(jnp.swapaxes(w_in, 1, 2))
    wa_pad = jnp.zeros((depth, LANE, QK_W), F32)
    wa_pad = wa_pad.at[:, SM_GA:SM_GA + GLA_RANK, :].set(w_gla_a).astype(BF16)
    gate_bias = jnp.zeros((depth, 1, LANE), F32)
    gate_bias = gate_bias.at[:, 0, SM_MI:SM_MI + N_HEADS].set(b_ml_i)
    gate_bias = gate_bias.at[:, 0, SM_MF:SM_MF + N_HEADS].set(b_ml_f)
    tri = chunk_tri(t_blk)
    tri_f = tri.astype(F32)
    rep_sel = gate_replicator()
    b_gla_a3 = b_gla_a.reshape(depth, 1, QK_W)
    gla_norm3 = gla_norm.reshape(depth, 1, V_W)
    conv_b3 = conv_b.reshape(depth, 1, 2 * QK_W)
    ml_norm3 = ml_norm.reshape(depth, 1, V_W)

    memn = rmsnorm(mem.reshape(MEM_LEN, d), mem_norm, BF16, tm=MEM_LEN)

    for l in range(depth):
        h = rmsnorm(xs, norm_mix[l], BF16)
        z, = matmul_rows(h, w_in_pt, l, tm=1024, tn=768, out_dtype=F32, transposed_b=True,
                         name="in_proj")
        km = matmul_cols(memn, w_mem_k, l, tm=MEM_LEN, tn=256, out_dtype=BF16, name="mem_k")
        vm = matmul_cols(memn, w_mem_v, l, tm=MEM_LEN, tn=256, out_dtype=BF16, name="mem_v")
        o_gla = gla_branch(z, tri, tri_f, wa_pad, b_gla_a3, gla_norm3, l, t_blk=t_blk)
        o_ml = mlstm_branch(z, tri, tri_f, rep_sel, conv_w, conv_b3, gate_bias, ml_norm3, l, t_blk=t_blk)
        o_mem = memattn_branch(z, km, vm)
        y, w_out_b, w_ff1_b = merge_branches(o_gla, o_ml, o_mem, z, w_branch, w_gate, b_gate, l,
                                             side_cast=(w_out, w_ff1))
        xs = matmul_acc(y, w_out_b, l, xs, tm=1024, tn=1024, tk=d, name="out_proj")
        h = rmsnorm(xs, norm_ffn[l], BF16)
        hid, w_ff2_b = matmul_rows(h, w_ff1_b, l, tm=1024, tn=1024, out_dtype=BF16,
                                   epilogue="relu2", side_cast=(w_ff2,), name="ffn_up")
        xs = matmul_acc(hid, w_ff2_b, l, xs, tm=1024, tn=1024, tk=4096, name="ffn_down")
    return rmsnorm(xs, final_norm, F32).reshape(x.shape)
```

```python
import functools

import jax
import jax.numpy as jnp
from jax import lax
from jax.experimental import pallas as pl
from jax.experimental.pallas import tpu as pltpu

F32 = jnp.float32
BF16 = jnp.bfloat16

EPS = 1e-6
CHUNK = 64
N_HEADS = 4
QK_DIM = 128
V_DIM = 256
QK_W = N_HEADS * QK_DIM
V_W = N_HEADS * V_DIM
GLA_RANK = 16
GLA_TAU = 16.0
CONV_W = 4
MEM_LEN = 256
GATE_RANK = 256
N_BRANCH = 3

LANE = 128
SUBLANE = 8
VMEM_LIMIT = 56 * 1024 * 1024

COL_GV, COL_GG, COL_MV, COL_MO, COL_XQ = 0, 8, 16, 24, 32
COL_MQK, COL_GQ, COL_GK = 40, 48, 52
COL_GATE, COL_SMALL, IN_COLS = 56, 58, 60
SM_GA, SM_MI, SM_MF = 0, 16, 20


def _cparams(sem):
    return pltpu.CompilerParams(dimension_semantics=sem, vmem_limit_bytes=VMEM_LIMIT)


def _col_block(col, width):
    return col * LANE // width


def _rmsnorm_kernel(x_ref, g_ref, o_ref):
    x = x_ref[...]
    ms = jnp.mean(x * x, axis=-1, keepdims=True)
    o_ref[...] = ((x * lax.rsqrt(ms + EPS)) * g_ref[...]).astype(o_ref.dtype)


def rmsnorm(x, g, out_dtype, tm=256):
    m, d = x.shape
    tm = min(tm, m)
    return pl.pallas_call(
        _rmsnorm_kernel,
        grid=(m // tm,),
        in_specs=[pl.BlockSpec((tm, d), lambda i: (i, 0)),
                  pl.BlockSpec((1, d), lambda i: (0, 0))],
        out_specs=pl.BlockSpec((tm, d), lambda i: (i, 0)),
        out_shape=jax.ShapeDtypeStruct((m, d), out_dtype),
        compiler_params=_cparams(("parallel",)),
        name="rmsnorm",
    )(x, g.reshape(1, d))


def _apply_epilogue(acc, epilogue):
    if epilogue == "relu2":
        return jnp.square(jnp.maximum(acc, 0.0))
    return acc


def _side_cast_specs(srcs, layer, n_steps, step_of):
    in_specs, out_specs, out_shapes = [], [], []
    for w in srcs:
        r, c = w.shape[-2:]
        rows = r // n_steps
        in_specs.append(pl.BlockSpec((None, rows, c), lambda *g: (layer, step_of(*g), 0)))
        out_specs.append(pl.BlockSpec((rows, c), lambda *g: (step_of(*g), 0)))
        out_shapes.append(jax.ShapeDtypeStruct((r, c), BF16))
    return in_specs, out_specs, out_shapes


def _side_cast(side_in, side_out):
    for src_ref, dst_ref in zip(side_in, side_out):
        dst_ref[...] = src_ref[...].astype(dst_ref.dtype)


def _weight_spec(b, layer, block, index):
    if b.ndim == 2:
        return pl.BlockSpec(block, index)
    return pl.BlockSpec((None,) + block, lambda *g: (layer,) + index(*g))


def _norm_mm_rows_kernel(*refs, transposed_b, epilogue, n_side, ts, n_slab):
    x_ref, g_ref, b_ref = refs[:3]
    side_in = refs[3:3 + n_side]
    o_ref = refs[3 + n_side]
    side_out = refs[4 + n_side:4 + 2 * n_side]
    h_even, h_odd = refs[4 + 2 * n_side:]
    r = pl.program_id(0)
    slab = jnp.minimum(pl.program_id(1), n_slab - 1)
    dims = (((1,), (1 if transposed_b else 0,)), ((), ()))

    def step(read_ref, write_ref):
        if read_ref is not None:
            acc = lax.dot_general(read_ref[...], b_ref[...], dims, preferred_element_type=F32)
            o_ref[...] = _apply_epilogue(acc, epilogue).astype(o_ref.dtype)
        x = x_ref[...]
        ms = jnp.mean(x * x, axis=-1, keepdims=True)
        h = ((x * lax.rsqrt(ms + EPS)) * g_ref[...]).astype(write_ref.dtype)
        write_ref[pl.ds(pl.multiple_of(slab * ts, ts), ts), :] = h
        _side_cast(side_in, side_out)

    @pl.when(r == 0)
    def _():
        step(None, h_even)

    @pl.when(r % 2 == 1)
    def _():
        step(h_even, h_odd)

    @pl.when(jnp.logical_and(r > 0, r % 2 == 0))
    def _():
        step(h_odd, h_even)


def norm_matmul_rows(x, g, b, layer, *, tm, tn, out_dtype, name, epilogue=None,
                     transposed_b=False, side_cast=()):
    m, kdim = x.shape
    n = b.shape[-2] if transposed_b else b.shape[-1]
    tm = min(tm, m)
    ni, nj = m // tm, n // tn
    n_slab = 1 << (nj.bit_length() - 1)
    ts = tm // n_slab
    b_spec = (_weight_spec(b, layer, (tn, kdim), lambda r, j: (j, 0)) if transposed_b
              else _weight_spec(b, layer, (kdim, tn), lambda r, j: (0, j)))
    s_in, s_out, s_shapes = _side_cast_specs(
        side_cast, layer, ni * nj, lambda r, j: jnp.where(r == 0, 0, (r - 1) * nj + j))
    x_spec = pl.BlockSpec(
        (ts, kdim),
        lambda r, j: (jnp.minimum(r, ni - 1) * n_slab + jnp.minimum(j, n_slab - 1), 0))
    return pl.pallas_call(
        functools.partial(_norm_mm_rows_kernel, transposed_b=transposed_b, epilogue=epilogue,
                          n_side=len(side_cast), ts=ts, n_slab=n_slab),
        grid=(ni + 1, nj),
        in_specs=[x_spec, pl.BlockSpec((1, kdim), lambda r, j: (0, 0)), b_spec] + s_in,
        out_specs=[pl.BlockSpec((tm, tn), lambda r, j: (jnp.maximum(r - 1, 0),
                                                        jnp.where(r == 0, 0, j)))] + s_out,
        out_shape=[jax.ShapeDtypeStruct((m, n), out_dtype)] + s_shapes,
        scratch_shapes=[pltpu.VMEM((tm, kdim), BF16), pltpu.VMEM((tm, kdim), BF16)],
        compiler_params=_cparams(("arbitrary", "arbitrary")),
        name=name,
    )(x, g.reshape(1, kdim), b, *side_cast)


def _mm_rows_kernel(*refs, transposed_b, epilogue, n_side):
    a_ref, b_ref = refs[:2]
    side_in = refs[2:2 + n_side]
    o_ref = refs[2 + n_side]
    side_out = refs[3 + n_side:3 + 2 * n_side]
    dims = (((1,), (1 if transposed_b else 0,)), ((), ()))
    acc = lax.dot_general(a_ref[...], b_ref[...], dims, preferred_element_type=F32)
    o_ref[...] = _apply_epilogue(acc, epilogue).astype(o_ref.dtype)
    _side_cast(side_in, side_out)


def matmul_rows(a, b, layer, *, tm, tn, out_dtype, name, epilogue=None, transposed_b=False,
                side_cast=()):
    m, kdim = a.shape
    n = b.shape[-2] if transposed_b else b.shape[-1]
    tm = min(tm, m)
    grid = (m // tm, n // tn)
    b_spec = (_weight_spec(b, layer, (tn, kdim), lambda i, j: (j, 0)) if transposed_b
              else _weight_spec(b, layer, (kdim, tn), lambda i, j: (0, j)))
    s_in, s_out, s_shapes = _side_cast_specs(side_cast, layer, grid[0] * grid[1],
                                             lambda i, j: i * grid[1] + j)
    return pl.pallas_call(
        functools.partial(_mm_rows_kernel, transposed_b=transposed_b, epilogue=epilogue,
                          n_side=len(side_cast)),
        grid=grid,
        in_specs=[pl.BlockSpec((tm, kdim), lambda i, j: (i, 0)), b_spec] + s_in,
        out_specs=[pl.BlockSpec((tm, tn), lambda i, j: (i, j))] + s_out,
        out_shape=[jax.ShapeDtypeStruct((m, n), out_dtype)] + s_shapes,
        compiler_params=_cparams(("parallel", "parallel")),
        name=name,
    )(a, b, *side_cast)


def _mm_cols_kernel(a_ref, b_ref, o_ref, wb_ref, *, epilogue):
    @pl.when(pl.program_id(1) == 0)
    def _():
        wb_ref[...] = b_ref[...].astype(BF16)

    acc = jnp.dot(a_ref[...].astype(BF16), wb_ref[...], preferred_element_type=F32)
    o_ref[...] = _apply_epilogue(acc, epilogue).astype(o_ref.dtype)


def matmul_cols(a, b, layer, *, tm, tn, out_dtype, epilogue=None, name):
    m, kdim = a.shape
    n = b.shape[-1]
    tm = min(tm, m)
    return pl.pallas_call(
        functools.partial(_mm_cols_kernel, epilogue=epilogue),
        grid=(n // tn, m // tm),
        in_specs=[pl.BlockSpec((tm, kdim), lambda j, i: (i, 0)),
                  pl.BlockSpec((None, kdim, tn), lambda j, i: (layer, 0, j))],
        out_specs=pl.BlockSpec((tm, tn), lambda j, i: (i, j)),
        out_shape=jax.ShapeDtypeStruct((m, n), out_dtype),
        scratch_shapes=[pltpu.VMEM((kdim, tn), BF16)],
        compiler_params=_cparams(("parallel", "arbitrary")),
        name=name,
    )(a, b)


def _mm_acc_kernel(a_ref, b_ref, r_ref, o_ref):
    @pl.when(pl.program_id(2) == 0)
    def _():
        o_ref[...] = r_ref[...]

    o_ref[...] = jnp.dot(a_ref[...], b_ref[...], preferred_element_type=F32) + o_ref[...]


def matmul_acc(a, b, layer, residual, *, tm, tn, tk, name):
    m, kdim = a.shape
    n = b.shape[-1]
    tm = min(tm, m)
    return pl.pallas_call(
        _mm_acc_kernel,
        grid=(m // tm, n // tn, kdim // tk),
        in_specs=[pl.BlockSpec((tm, tk), lambda i, j, k: (i, k)),
                  _weight_spec(b, layer, (tk, tn), lambda i, j, k: (k, j)),
                  pl.BlockSpec((tm, tn), lambda i, j, k: (i, j))],
        out_specs=pl.BlockSpec((tm, tn), lambda i, j, k: (i, j)),
        out_shape=jax.ShapeDtypeStruct((m, n), F32),
        compiler_params=_cparams(("parallel", "parallel", "arbitrary")),
        name=name,
    )(a, b, residual)


def _log_sigmoid(x):
    return jnp.minimum(x, 0.0) - jnp.log1p(jnp.exp(-jnp.abs(x)))


def _sigmoid(x):
    return 1.0 / (1.0 + jnp.exp(-x))


def chunk_tri(t):
    r = lax.broadcasted_iota(jnp.int32, (t, t), 0)
    c = lax.broadcasted_iota(jnp.int32, (t, t), 1)
    same = (r // CHUNK) == (c // CHUNK)
    return jnp.logical_and(same, c <= r).astype(BF16)


def _chunk_cumsum(x, tri):
    hi = x.astype(BF16)
    r1 = x - hi.astype(F32)
    mid = r1.astype(BF16)
    lo = (r1 - mid.astype(F32)).astype(BF16)
    out = jnp.dot(tri, hi, preferred_element_type=F32)
    out += jnp.dot(tri, mid, preferred_element_type=F32)
    out += jnp.dot(tri, lo, preferred_element_type=F32)
    return out


def _split_dot(x, sel):
    hi = x.astype(BF16)
    r1 = x - hi.astype(F32)
    mid = r1.astype(BF16)
    lo = (r1 - mid.astype(F32)).astype(BF16)
    out = jnp.dot(hi, sel, preferred_element_type=F32)
    out += jnp.dot(mid, sel, preferred_element_type=F32)
    out += jnp.dot(lo, sel, preferred_element_type=F32)
    return out


def gate_replicator():
    src = lax.broadcasted_iota(jnp.int32, (LANE, 2 * N_HEADS * LANE), 0)
    grp = lax.broadcasted_iota(jnp.int32, (LANE, 2 * N_HEADS * LANE), 1) // LANE
    want = jnp.where(grp % 2 == 0, SM_MI, SM_MF) + grp // 2
    return (src == want).astype(BF16)


def _dot_nt(a, b):
    return lax.dot_general(a, b, (((1,), (1,)), ((), ())), preferred_element_type=F32)


def _dot_tn(a, b):
    return lax.dot_general(a, b, (((0,), (0,)), ((), ())), preferred_element_type=F32)


def _head_rmsnorm(o, g):
    ms = jnp.mean(o * o, axis=-1, keepdims=True)
    return (o * lax.rsqrt(ms + EPS)) * g


def _gla_kernel(q_ref, k_ref, v_ref, gg_ref, sm_ref, tri_ref, mask_ref, wa_ref, ba_ref, gn_ref,
                o_ref, st_ref, *, t_blk):
    @pl.when(pl.program_id(0) == 0)
    def _():
        st_ref[...] = jnp.zeros_like(st_ref)

    pre = jnp.dot(sm_ref[...].astype(BF16), wa_ref[...],
                  preferred_element_type=F32) + ba_ref[...]
    log_a = _log_sigmoid(pre) / GLA_TAU
    b_all = _chunk_cumsum(log_a, tri_ref[...])
    allowed = mask_ref[...] > 0.5
    scale = QK_DIM ** -0.5
    n_chunks = t_blk // CHUNK
    chunk_rows = [slice(c * CHUNK, (c + 1) * CHUNK) for c in range(n_chunks)]
    heads = range(N_HEADS)
    vcs = [slice(h * V_DIM, (h + 1) * V_DIM) for h in heads]

    q_d, o_intra, d_st, decay = [], [], [], []
    for h in heads:
        kc = slice(h * QK_DIM, (h + 1) * QK_DIM)
        b = b_all[:, kc]
        b_last = [b[r][CHUNK - 1:CHUNK] for r in chunk_rows]
        b_last_rows = jnp.concatenate(
            [jnp.broadcast_to(bl, (CHUNK, QK_DIM)) for bl in b_last], axis=0)
        k = k_ref[:, kc]
        v = v_ref[:, vcs[h]].astype(BF16)
        qd = ((q_ref[:, kc] * scale) * jnp.exp(b)).astype(BF16)
        k_d = (k * jnp.exp(-b)).astype(BF16)
        k_s = (k * jnp.exp(b_last_rows - b)).astype(BF16)
        att = jnp.where(allowed, _dot_nt(qd, k_d), 0.0)
        q_d.append(qd)
        o_intra.append(jnp.dot(att.astype(BF16), v, preferred_element_type=F32))
        d_st.append([_dot_tn(v[r], k_s[r]) for r in chunk_rows])
        decay.append([jnp.exp(bl) for bl in b_last])

    states = [st_ref[h] for h in heads]
    inter = [[] for _ in heads]
    for c, r in enumerate(chunk_rows):
        for h in heads:
            inter[h].append(_dot_nt(q_d[h][r], states[h].astype(BF16)))
            states[h] = states[h] * decay[h][c] + d_st[h][c]
    for h in heads:
        st_ref[h] = states[h]

    for h in heads:
        o = o_intra[h] + jnp.concatenate(inter[h], axis=0)
        gg = gg_ref[:, vcs[h]]
        o = _head_rmsnorm(o, gn_ref[:, vcs[h]]) * (gg * _sigmoid(gg))
        o_ref[:, vcs[h]] = o.astype(o_ref.dtype)


def gla_branch(z, tri, mask, wa_pad, ba, gn, layer, *, t_blk=256):
    s = z.shape[0]
    return pl.pallas_call(
        functools.partial(_gla_kernel, t_blk=t_blk),
        grid=(s // t_blk,),
        in_specs=[
            pl.BlockSpec((t_blk, QK_W), lambda t: (t, _col_block(COL_GQ, QK_W))),
            pl.BlockSpec((t_blk, QK_W), lambda t: (t, _col_block(COL_GK, QK_W))),
            pl.BlockSpec((t_blk, V_W), lambda t: (t, _col_block(COL_GV, V_W))),
            pl.BlockSpec((t_blk, V_W), lambda t: (t, _col_block(COL_GG, V_W))),
            pl.BlockSpec((t_blk, LANE), lambda t: (t, COL_SMALL)),
            pl.BlockSpec((t_blk, t_blk), lambda t: (0, 0)),
            pl.BlockSpec((t_blk, t_blk), lambda t: (0, 0)),
            pl.BlockSpec((None, LANE, QK_W), lambda t: (layer, 0, 0)),
            pl.BlockSpec((None, 1, QK_W), lambda t: (layer, 0, 0)),
            pl.BlockSpec((None, 1, V_W), lambda t: (layer, 0, 0)),
        ],
        out_specs=pl.BlockSpec((t_blk, V_W), lambda t: (t, 0)),
        out_shape=jax.ShapeDtypeStruct((s, V_W), BF16),
        scratch_shapes=[pltpu.VMEM((N_HEADS, V_DIM, QK_DIM), F32)],
        compiler_params=_cparams(("arbitrary",)),
        name="gla",
    )(z, z, z, z, z, tri, mask, wa_pad, ba, gn)


def _mlstm_kernel(qk_ref, v_ref, mo_ref, sm_ref, tri_ref, mask_ref, rep_ref, cw_ref, cb_ref, gb_ref, mn_ref,
                  o_ref, u_ref, c_ref, n_ref, m_ref, *, t_blk):
    @pl.when(pl.program_id(0) == 0)
    def _():
        u_ref[0:SUBLANE, :] = jnp.zeros((SUBLANE, 2 * QK_W), F32)
        c_ref[...] = jnp.zeros_like(c_ref)
        n_ref[...] = jnp.zeros_like(n_ref)
        m_ref[...] = jnp.full(m_ref.shape, -1e30, F32)

    u_ref[SUBLANE:SUBLANE + t_blk, :] = qk_ref[...]
    y = cb_ref[...]
    for j in range(CONV_W):
        off = SUBLANE - (CONV_W - 1) + j
        y = y + cw_ref[j:j + 1, :] * u_ref[off:off + t_blk, :]
    u_ref[0:SUBLANE, :] = u_ref[t_blk:t_blk + SUBLANE, :]
    qk_all = y * _sigmoid(y)

    pre = sm_ref[...] + gb_ref[...]
    f_cum = _chunk_cumsum(_log_sigmoid(pre), tri_ref[...])
    lane = lax.broadcasted_iota(jnp.int32, (t_blk, LANE), 1)
    is_f = jnp.logical_and(lane >= SM_MF, lane < SM_MF + N_HEADS)
    cols = jnp.where(is_f, f_cum, pre)
    rows_t = cols.T

    allowed = mask_ref[...] > 0.5
    n_chunks = t_blk // CHUNK
    chunk_rows = [slice(c * CHUNK, (c + 1) * CHUNK) for c in range(n_chunks)]
    row_chunk = lax.broadcasted_iota(jnp.int32, (t_blk, LANE), 0) // CHUNK
    lane_chunk = lax.broadcasted_iota(jnp.int32, (1, t_blk), 1) // CHUNK
    rep = _split_dot(cols, rep_ref[...])

    def per_chunk_rep(vals):
        out = jnp.broadcast_to(vals[-1], (t_blk, LANE))
        for c in range(n_chunks - 2, -1, -1):
            out = jnp.where(row_chunk == c, vals[c], out)
        return out

    def lanes2(x):
        return jnp.concatenate([x] * (V_DIM // LANE), axis=1)

    k_scale = QK_DIM ** -0.5
    heads = range(N_HEADS)
    vcs = [slice(h * V_DIM, (h + 1) * V_DIM) for h in heads]
    q = [qk_all[:, h * QK_DIM:(h + 1) * QK_DIM] for h in heads]
    k = [qk_all[:, QK_W + h * QK_DIM:QK_W + (h + 1) * QK_DIM] * k_scale for h in heads]
    qb = [x.astype(BF16) for x in q]
    v = [v_ref[:, vc].astype(BF16) for vc in vcs]
    li_rep = [rep[:, (2 * h) * LANE:(2 * h + 1) * LANE] for h in heads]
    f_rep = [rep[:, (2 * h + 1) * LANE:(2 * h + 2) * LANE] for h in heads]
    lir = [rows_t[SM_MI + h:SM_MI + h + 1, :] for h in heads]
    fr = [rows_t[SM_MF + h:SM_MF + h + 1, :] for h in heads]

    f_last, g, d_c, d_n, lmat, l_max, qk = [], [], [], [], [], [], []
    for h in heads:
        fl = [fr[h][:, (c + 1) * CHUNK - 1:(c + 1) * CHUNK] for c in range(n_chunks)]
        fl_row = fl[-1]
        for c in range(n_chunks - 2, -1, -1):
            fl_row = jnp.where(lane_chunk == c, fl[c], fl_row)
        g_row = fl_row - fr[h] + lir[h]
        gh = [jnp.max(jnp.where(lane_chunk == c, g_row, -jnp.inf), axis=-1, keepdims=True)
              for c in range(n_chunks)]
        kw = k[h] * jnp.exp(per_chunk_rep(fl) - f_rep[h] + li_rep[h] - per_chunk_rep(gh))
        kwb = kw.astype(BF16)
        f_last.append(fl)
        g.append(gh)
        d_c.append([_dot_tn(kwb[r], v[h][r]) for r in chunk_rows])
        d_n.append([jnp.sum(kw[r], axis=0, keepdims=True) for r in chunk_rows])
        lm = jnp.where(allowed, lanes2(f_rep[h]) - fr[h] + lir[h], -jnp.inf)
        lmat.append(lm)
        l_max.append(jnp.max(lm, axis=-1, keepdims=True))
        qk.append(_dot_nt(qb[h], k[h].astype(BF16)))

    m_prev = [m_ref[h][:, 0:1] for h in heads]
    c_prev = [c_ref[h] for h in heads]
    n_prev = [n_ref[h] for h in heads]
    m_in = [[] for _ in heads]
    inter = [[] for _ in heads]
    qn = [[] for _ in heads]
    for c, r in enumerate(chunk_rows):
        for h in heads:
            m_in[h].append(m_prev[h])
            inter[h].append(jnp.dot(qb[h][r], c_prev[h].astype(BF16),
                                    preferred_element_type=F32))
            qn[h].append(jnp.sum(q[h][r] * n_prev[h], axis=-1, keepdims=True))
            m_new = jnp.maximum(f_last[h][c] + m_prev[h], g[h][c])
            a = jnp.exp(f_last[h][c] + m_prev[h] - m_new)
            bb = jnp.exp(g[h][c] - m_new)
            c_prev[h] = a * c_prev[h] + bb * d_c[h][c]
            n_prev[h] = a * n_prev[h] + bb * d_n[h][c]
            m_prev[h] = m_new
    for h in heads:
        c_ref[h] = c_prev[h]
        n_ref[h] = n_prev[h]
        m_ref[h] = jnp.broadcast_to(m_prev[h], (1, LANE))

    for h in heads:
        m_inter = f_rep[h] + per_chunk_rep(m_in[h])
        m = jnp.maximum(l_max[h], m_inter)
        s_mat = qk[h] * jnp.exp(lmat[h] - lanes2(m))
        w_inter = jnp.exp(m_inter - m)
        num = jnp.dot(s_mat.astype(BF16), v[h], preferred_element_type=F32)
        num += lanes2(w_inter) * jnp.concatenate(inter[h], axis=0)
        den = jnp.sum(s_mat, axis=-1, keepdims=True)
        den = den + w_inter * jnp.concatenate(qn[h], axis=0)
        hid = num / lanes2(jnp.maximum(jnp.abs(den), jnp.exp(-m)))
        o = _sigmoid(mo_ref[:, vcs[h]]) * hid
        o_ref[:, vcs[h]] = _head_rmsnorm(o, mn_ref[:, vcs[h]]).astype(o_ref.dtype)


def mlstm_branch(z, tri, mask, rep_sel, conv_w, conv_b, gate_bias, mn, layer, *, t_blk=256):
    s = z.shape[0]
    return pl.pallas_call(
        functools.partial(_mlstm_kernel, t_blk=t_blk),
        grid=(s // t_blk,),
        in_specs=[
            pl.BlockSpec((t_blk, 2 * QK_W), lambda t: (t, _col_block(COL_MQK, 2 * QK_W))),
            pl.BlockSpec((t_blk, V_W), lambda t: (t, _col_block(COL_MV, V_W))),
            pl.BlockSpec((t_blk, V_W), lambda t: (t, _col_block(COL_MO, V_W))),
            pl.BlockSpec((t_blk, LANE), lambda t: (t, COL_SMALL)),
            pl.BlockSpec((t_blk, t_blk), lambda t: (0, 0)),
            pl.BlockSpec((t_blk, t_blk), lambda t: (0, 0)),
            pl.BlockSpec((LANE, 2 * N_HEADS * LANE), lambda t: (0, 0)),
            pl.BlockSpec((None, CONV_W, 2 * QK_W), lambda t: (layer, 0, 0)),
            pl.BlockSpec((None, 1, 2 * QK_W), lambda t: (layer, 0, 0)),
            pl.BlockSpec((None, 1, LANE), lambda t: (layer, 0, 0)),
            pl.BlockSpec((None, 1, V_W), lambda t: (layer, 0, 0)),
        ],
        out_specs=pl.BlockSpec((t_blk, V_W), lambda t: (t, 0)),
        out_shape=jax.ShapeDtypeStruct((s, V_W), BF16),
        scratch_shapes=[pltpu.VMEM((SUBLANE + t_blk, 2 * QK_W), F32),
                        pltpu.VMEM((N_HEADS, QK_DIM, V_DIM), F32),
                        pltpu.VMEM((N_HEADS, 1, QK_DIM), F32),
                        pltpu.VMEM((N_HEADS, 1, LANE), F32)],
        compiler_params=_cparams(("arbitrary",)),
        name="mlstm",
    )(z, z, z, z, tri, mask, rep_sel, conv_w, conv_b, gate_bias, mn)


def _memattn_kernel(q_ref, km_ref, vm_ref, o_ref):
    scale = V_DIM ** -0.5
    for h in range(N_HEADS):
        cols = slice(h * V_DIM, (h + 1) * V_DIM)
        q = q_ref[:, cols].astype(BF16)
        s = _dot_nt(q, km_ref[:, cols]) * scale
        e = jnp.exp(s - jnp.max(s, axis=-1, keepdims=True))
        p = e / jnp.sum(e, axis=-1, keepdims=True)
        o = jnp.dot(p.astype(BF16), vm_ref[:, cols], preferred_element_type=F32)
        o_ref[:, cols] = o.astype(o_ref.dtype)


def memattn_branch(z, km, vm, *, tm=512):
    s = z.shape[0]
    tm = min(tm, s)
    return pl.pallas_call(
        _memattn_kernel,
        grid=(s // tm,),
        in_specs=[pl.BlockSpec((tm, V_W), lambda i: (i, _col_block(COL_XQ, V_W))),
                  pl.BlockSpec((MEM_LEN, V_W), lambda i: (0, 0)),
                  pl.BlockSpec((MEM_LEN, V_W), lambda i: (0, 0))],
        out_specs=pl.BlockSpec((tm, V_W), lambda i: (i, 0)),
        out_shape=jax.ShapeDtypeStruct((s, V_W), BF16),
        compiler_params=_cparams(("parallel",)),
        name="memattn",
    )(z, km, vm)


def _merge_kernel(*refs, n_side):
    o1_ref, o2_ref, o3_ref, gl_ref, wb_ref, wg_ref, bg_ref = refs[:7]
    side_in = refs[7:7 + n_side]
    y_ref = refs[7 + n_side]
    side_out = refs[8 + n_side:8 + 2 * n_side]
    wbb_ref, wgb_ref = refs[8 + 2 * n_side:]
    _side_cast(side_in, side_out)

    @pl.when(pl.program_id(1) == 0)
    def _():
        wbb_ref[...] = wb_ref[...].astype(BF16)
        wgb_ref[...] = wg_ref[...].astype(BF16)

    gl = gl_ref[...].astype(BF16)
    y = None
    for j, o_ref in enumerate((o1_ref, o2_ref, o3_ref)):
        gate = _sigmoid(jnp.dot(gl, wgb_ref[j], preferred_element_type=F32)
                        + bg_ref[j:j + 1, :])
        term = gate * jnp.dot(o_ref[...], wbb_ref[j], preferred_element_type=F32)
        y = term if y is None else y + term
    y_ref[...] = y.astype(y_ref.dtype)


def merge_branches(o1, o2, o3, z, wb, wg, bg, layer, *, tm=1024, tn=512, side_cast=()):
    s, w = o1.shape
    d = wb.shape[-1]
    tm = min(tm, s)
    grid = (d // tn, s // tm)
    o_spec = pl.BlockSpec((tm, w), lambda j, i: (i, 0))
    s_in, s_out, s_shapes = _side_cast_specs(side_cast, layer, grid[0] * grid[1],
                                             lambda j, i: j * grid[1] + i)
    return pl.pallas_call(
        functools.partial(_merge_kernel, n_side=len(side_cast)),
        grid=grid,
        in_specs=[o_spec, o_spec, o_spec,
                  pl.BlockSpec((tm, GATE_RANK), lambda j, i: (i, _col_block(COL_GATE, GATE_RANK))),
                  pl.BlockSpec((None, N_BRANCH, w, tn), lambda j, i: (layer, 0, 0, j)),
                  pl.BlockSpec((None, N_BRANCH, GATE_RANK, tn), lambda j, i: (layer, 0, 0, j)),
                  pl.BlockSpec((None, N_BRANCH, tn), lambda j, i: (layer, 0, j))] + s_in,
        out_specs=[pl.BlockSpec((tm, tn), lambda j, i: (i, j))] + s_out,
        out_shape=[jax.ShapeDtypeStruct((s, d), BF16)] + s_shapes,
        scratch_shapes=[pltpu.VMEM((N_BRANCH, w, tn), BF16),
                        pltpu.VMEM((N_BRANCH, GATE_RANK, tn), BF16)],
        compiler_params=_cparams(("parallel", "arbitrary")),
        name="merge",
    )(o1, o2, o3, z, wb, wg, bg, *side_cast)


_IN_GROUPS = ("gq", "gk", "gv", "gg", "ga", "mqk", "mv", "mo", "mi", "mf", "xq", "gate")
_IN_SIZES = (QK_W, QK_W, V_W, V_W, GLA_RANK, 2 * QK_W, V_W, V_W, N_HEADS, N_HEADS, V_W, GATE_RANK)
_IN_SRC = {name: sum(_IN_SIZES[:i]) for i, name in enumerate(_IN_GROUPS)}
_IN_WIDTH = dict(zip(_IN_GROUPS, _IN_SIZES))
_IN_DST = {"gv": COL_GV, "gg": COL_GG, "mv": COL_MV, "mo": COL_MO, "xq": COL_XQ,
           "mqk": COL_MQK, "gq": COL_GQ, "gk": COL_GK, "gate": COL_GATE}
_IN_SMALL = ("ga", "mi", "mf")
assert (SM_GA, SM_MI, SM_MF) == (0, GLA_RANK, GLA_RANK + N_HEADS)


def _permute_wt_kernel(w_ref, o_ref):
    lanes = w_ref.shape[1]
    for name, col in _IN_DST.items():
        src, width = _IN_SRC[name], _IN_WIDTH[name]
        o_ref[col * LANE:col * LANE + width, :] = w_ref[src:src + width, :].astype(o_ref.dtype)
    spans = []
    for n in _IN_SMALL:
        if spans and spans[-1][0] + spans[-1][1] == _IN_SRC[n]:
            spans[-1] = (spans[-1][0], spans[-1][1] + _IN_WIDTH[n])
        else:
            spans.append((_IN_SRC[n], _IN_WIDTH[n]))
    small = [w_ref[src:src + width, :] for src, width in spans]
    used = sum(width for _, width in spans)
    small.append(jnp.zeros(((IN_COLS - COL_SMALL) * LANE - used, lanes), F32))
    o_ref[COL_SMALL * LANE:, :] = jnp.concatenate(small, axis=0).astype(o_ref.dtype)


def permute_w_in_t(w_in_t, *, tc=256):
    depth, n_src, d = w_in_t.shape
    n_dst = IN_COLS * LANE
    return pl.pallas_call(
        _permute_wt_kernel,
        grid=(depth, d // tc),
        in_specs=[pl.BlockSpec((None, n_src, tc), lambda l, i: (l, 0, i))],
        out_specs=pl.BlockSpec((None, n_dst, tc), lambda l, i: (l, 0, i)),
        out_shape=jax.ShapeDtypeStruct((depth, n_dst, d), BF16),
        compiler_params=_cparams(("parallel", "parallel")),
        name="permute_w_in",
    )(w_in_t)


def kernel(x, mem, norm_mix, w_in, w_gla_a, b_gla_a, gla_norm, conv_w, conv_b, b_ml_i, b_ml_f, ml_norm, mem_norm, w_mem_k, w_mem_v, w_branch, w_gate, b_gate, w_out, norm_ffn, w_ff1, w_ff2, final_norm):
    depth = w_in.shape[0]
    _, s, d = x.shape
    xs = x.reshape(s, d)
    t_blk = min(256, s)

    w_in_pt = permute_w_in_t(jnp.swapaxes(w_in, 1, 2))
    wa_pad = jnp.zeros((depth, LANE, QK_W), F32)
    wa_pad = wa_pad.at[:, SM_GA:SM_GA + GLA_RANK, :].set(w_gla_a).astype(BF16)
    gate_bias = jnp.zeros((depth, 1, LANE), F32)
    gate_bias = gate_bias.at[:, 0, SM_MI:SM_MI + N_HEADS].set(b_ml_i)
    gate_bias = gate_bias.at[:, 0, SM_MF:SM_MF + N_HEADS].set(b_ml_f)
    tri = chunk_tri(t_blk)
    tri_f = tri.astype(F32)
    rep_sel = gate_replicator()
    b_gla_a3 = b_gla_a.reshape(depth, 1, QK_W)
    gla_norm3 = gla_norm.reshape(depth, 1, V_W)
    conv_b3 = conv_b.reshape(depth, 1, 2 * QK_W)
    ml_norm3 = ml_norm.reshape(depth, 1, V_W)

    memn = rmsnorm(mem.reshape(MEM_LEN, d), mem_norm, BF16, tm=MEM_LEN)

    for l in range(depth):
        z, = norm_matmul_rows(xs, norm_mix[l], w_in_pt, l, tm=1024, tn=768, out_dtype=F32,
                              transposed_b=True, name="in_proj")
        km = matmul_cols(memn, w_mem_k, l, tm=MEM_LEN, tn=256, out_dtype=BF16, name="mem_k")
        vm = matmul_cols(memn, w_mem_v, l, tm=MEM_LEN, tn=256, out_dtype=BF16, name="mem_v")
        o_gla = gla_branch(z, tri, tri_f, wa_pad, b_gla_a3, gla_norm3, l, t_blk=t_blk)
        o_ml = mlstm_branch(z, tri, tri_f, rep_sel, conv_w, conv_b3, gate_bias, ml_norm3, l, t_blk=t_blk)
        o_mem = memattn_branch(z, km, vm)
        y, w_out_b, w_ff1_b = merge_branches(o_gla, o_ml, o_mem, z, w_branch, w_gate, b_gate, l,
                                             side_cast=(w_out, w_ff1))
        xs = matmul_acc(y, w_out_b, l, xs, tm=1024, tn=1024, tk=d, name="out_proj")
        hid, w_ff2_b = norm_matmul_rows(xs, norm_ffn[l], w_ff1_b, l, tm=1024, tn=1024,
                                        out_dtype=BF16, epilogue="relu2", side_cast=(w_ff2,),
                                        name="ffn_up")
        xs = matmul_acc(hid, w_ff2_b, l, xs, tm=1024, tn=1024, tk=4096, name="ffn_down")
    return rmsnorm(xs, final_norm, F32).reshape(x.shape)
```

```python
import functools

import jax
import jax.numpy as jnp
from jax import lax
from jax.experimental import pallas as pl
from jax.experimental.pallas import tpu as pltpu

F32 = jnp.float32
BF16 = jnp.bfloat16

EPS = 1e-6
CHUNK = 64
N_HEADS = 4
QK_DIM = 128
V_DIM = 256
QK_W = N_HEADS * QK_DIM
V_W = N_HEADS * V_DIM
GLA_RANK = 16
GLA_TAU = 16.0
CONV_W = 4
MEM_LEN = 256
GATE_RANK = 256
N_BRANCH = 3

LANE = 128
SUBLANE = 8
VMEM_LIMIT = 56 * 1024 * 1024

COL_GV, COL_GG, COL_MV, COL_MO, COL_XQ = 0, 8, 16, 24, 32
COL_MQK, COL_GQ, COL_GK = 40, 48, 52
COL_GATE, COL_SMALL, IN_COLS = 56, 58, 60
SM_GA, SM_MI, SM_MF = 0, 16, 20


def _cparams(sem):
    return pltpu.CompilerParams(dimension_semantics=sem, vmem_limit_bytes=VMEM_LIMIT)


def _col_block(col, width):
    return col * LANE // width


def _rmsnorm_kernel(x_ref, g_ref, o_ref):
    x = x_ref[...]
    ms = jnp.mean(x * x, axis=-1, keepdims=True)
    o_ref[...] = ((x * lax.rsqrt(ms + EPS)) * g_ref[...]).astype(o_ref.dtype)


def rmsnorm(x, g, out_dtype, tm=256):
    m, d = x.shape
    tm = min(tm, m)
    return pl.pallas_call(
        _rmsnorm_kernel,
        grid=(m // tm,),
        in_specs=[pl.BlockSpec((tm, d), lambda i: (i, 0)),
                  pl.BlockSpec((1, d), lambda i: (0, 0))],
        out_specs=pl.BlockSpec((tm, d), lambda i: (i, 0)),
        out_shape=jax.ShapeDtypeStruct((m, d), out_dtype),
        compiler_params=_cparams(("parallel",)),
        name="rmsnorm",
    )(x, g.reshape(1, d))


def _apply_epilogue(acc, epilogue):
    if epilogue == "relu2":
        return jnp.square(jnp.maximum(acc, 0.0))
    return acc


def _side_cast_specs(srcs, layer, n_steps, step_of):
    in_specs, out_specs, out_shapes = [], [], []
    for w in srcs:
        r, c = w.shape[-2:]
        rows = r // n_steps
        in_specs.append(pl.BlockSpec((None, rows, c), lambda *g: (layer, step_of(*g), 0)))
        out_specs.append(pl.BlockSpec((rows, c), lambda *g: (step_of(*g), 0)))
        out_shapes.append(jax.ShapeDtypeStruct((r, c), BF16))
    return in_specs, out_specs, out_shapes


def _side_cast(side_in, side_out):
    for src_ref, dst_ref in zip(side_in, side_out):
        dst_ref[...] = src_ref[...].astype(dst_ref.dtype)


def _weight_spec(b, layer, block, index):
    if b.ndim == 2:
        return pl.BlockSpec(block, index)
    return pl.BlockSpec((None,) + block, lambda *g: (layer,) + index(*g))


def _norm_mm_rows_kernel(*refs, transposed_b, epilogue, n_side, ts, n_slab):
    x_ref, g_ref, b_ref = refs[:3]
    side_in = refs[3:3 + n_side]
    o_ref = refs[3 + n_side]
    side_out = refs[4 + n_side:4 + 2 * n_side]
    h_even, h_odd = refs[4 + 2 * n_side:]
    r = pl.program_id(0)
    slab = jnp.minimum(pl.program_id(1), n_slab - 1)
    dims = (((1,), (1 if transposed_b else 0,)), ((), ()))

    def step(read_ref, write_ref):
        if read_ref is not None:
            acc = lax.dot_general(read_ref[...], b_ref[...], dims, preferred_element_type=F32)
            o_ref[...] = _apply_epilogue(acc, epilogue).astype(o_ref.dtype)
        x = x_ref[...]
        ms = jnp.mean(x * x, axis=-1, keepdims=True)
        h = ((x * lax.rsqrt(ms + EPS)) * g_ref[...]).astype(write_ref.dtype)
        write_ref[pl.ds(pl.multiple_of(slab * ts, ts), ts), :] = h
        _side_cast(side_in, side_out)

    @pl.when(r == 0)
    def _():
        step(None, h_even)

    @pl.when(r % 2 == 1)
    def _():
        step(h_even, h_odd)

    @pl.when(jnp.logical_and(r > 0, r % 2 == 0))
    def _():
        step(h_odd, h_even)


def norm_matmul_rows(x, g, b, layer, *, tm, tn, out_dtype, name, epilogue=None,
                     transposed_b=False, side_cast=()):
    m, kdim = x.shape
    n = b.shape[-2] if transposed_b else b.shape[-1]
    tm = min(tm, m)
    ni, nj = m // tm, n // tn
    n_slab = 1 << (nj.bit_length() - 1)
    ts = tm // n_slab
    def col(r, j):
        return jnp.where(r == 0, 0, j)

    b_spec = (_weight_spec(b, layer, (tn, kdim), lambda r, j: (col(r, j), 0)) if transposed_b
              else _weight_spec(b, layer, (kdim, tn), lambda r, j: (0, col(r, j))))
    s_in, s_out, s_shapes = _side_cast_specs(
        side_cast, layer, ni * nj, lambda r, j: jnp.where(r == 0, 0, (r - 1) * nj + j))
    x_spec = pl.BlockSpec(
        (ts, kdim),
        lambda r, j: (jnp.minimum(r, ni - 1) * n_slab + jnp.minimum(j, n_slab - 1), 0))
    return pl.pallas_call(
        functools.partial(_norm_mm_rows_kernel, transposed_b=transposed_b, epilogue=epilogue,
                          n_side=len(side_cast), ts=ts, n_slab=n_slab),
        grid=(ni + 1, nj),
        in_specs=[x_spec, pl.BlockSpec((1, kdim), lambda r, j: (0, 0)), b_spec] + s_in,
        out_specs=[pl.BlockSpec((tm, tn), lambda r, j: (jnp.maximum(r - 1, 0), col(r, j)))] + s_out,
        out_shape=[jax.ShapeDtypeStruct((m, n), out_dtype)] + s_shapes,
        scratch_shapes=[pltpu.VMEM((tm, kdim), BF16), pltpu.VMEM((tm, kdim), BF16)],
        compiler_params=_cparams(("arbitrary", "arbitrary")),
        name=name,
    )(x, g.reshape(1, kdim), b, *side_cast)


def _mm_rows_kernel(*refs, transposed_b, epilogue, n_side):
    a_ref, b_ref = refs[:2]
    side_in = refs[2:2 + n_side]
    o_ref = refs[2 + n_side]
    side_out = refs[3 + n_side:3 + 2 * n_side]
    dims = (((1,), (1 if transposed_b else 0,)), ((), ()))
    acc = lax.dot_general(a_ref[...], b_ref[...], dims, preferred_element_type=F32)
    o_ref[...] = _apply_epilogue(acc, epilogue).astype(o_ref.dtype)
    _side_cast(side_in, side_out)


def matmul_rows(a, b, layer, *, tm, tn, out_dtype, name, epilogue=None, transposed_b=False,
                side_cast=()):
    m, kdim = a.shape
    n = b.shape[-2] if transposed_b else b.shape[-1]
    tm = min(tm, m)
    grid = (m // tm, n // tn)
    b_spec = (_weight_spec(b, layer, (tn, kdim), lambda i, j: (j, 0)) if transposed_b
              else _weight_spec(b, layer, (kdim, tn), lambda i, j: (0, j)))
    s_in, s_out, s_shapes = _side_cast_specs(side_cast, layer, grid[0] * grid[1],
                                             lambda i, j: i * grid[1] + j)
    return pl.pallas_call(
        functools.partial(_mm_rows_kernel, transposed_b=transposed_b, epilogue=epilogue,
                          n_side=len(side_cast)),
        grid=grid,
        in_specs=[pl.BlockSpec((tm, kdim), lambda i, j: (i, 0)), b_spec] + s_in,
        out_specs=[pl.BlockSpec((tm, tn), lambda i, j: (i, j))] + s_out,
        out_shape=[jax.ShapeDtypeStruct((m, n), out_dtype)] + s_shapes,
        compiler_params=_cparams(("parallel", "parallel")),
        name=name,
    )(a, b, *side_cast)


def _mm_cols_kernel(a_ref, b_ref, o_ref, wb_ref, *, epilogue):
    @pl.when(pl.program_id(1) == 0)
    def _():
        wb_ref[...] = b_ref[...].astype(BF16)

    acc = jnp.dot(a_ref[...].astype(BF16), wb_ref[...], preferred_element_type=F32)
    o_ref[...] = _apply_epilogue(acc, epilogue).astype(o_ref.dtype)


def matmul_cols(a, b, layer, *, tm, tn, out_dtype, epilogue=None, name):
    m, kdim = a.shape
    n = b.shape[-1]
    tm = min(tm, m)
    return pl.pallas_call(
        functools.partial(_mm_cols_kernel, epilogue=epilogue),
        grid=(n // tn, m // tm),
        in_specs=[pl.BlockSpec((tm, kdim), lambda j, i: (i, 0)),
                  pl.BlockSpec((None, kdim, tn), lambda j, i: (layer, 0, j))],
        out_specs=pl.BlockSpec((tm, tn), lambda j, i: (i, j)),
        out_shape=jax.ShapeDtypeStruct((m, n), out_dtype),
        scratch_shapes=[pltpu.VMEM((kdim, tn), BF16)],
        compiler_params=_cparams(("parallel", "arbitrary")),
        name=name,
    )(a, b)


def _mm_acc_kernel(a_ref, b_ref, r_ref, o_ref, *, nk):
    def partial_sum():
        return jnp.dot(a_ref[...], b_ref[...], preferred_element_type=F32)

    if nk == 1:
        o_ref[...] = partial_sum() + r_ref[...]
        return
    k = pl.program_id(2)

    @pl.when(k == 0)
    def _():
        o_ref[...] = partial_sum()

    @pl.when(jnp.logical_and(k > 0, k < nk - 1))
    def _():
        o_ref[...] = partial_sum() + o_ref[...]

    @pl.when(k == nk - 1)
    def _():
        o_ref[...] = (partial_sum() + o_ref[...]) + r_ref[...]


def matmul_acc(a, b, layer, residual, *, tm, tn, tk, name):
    m, kdim = a.shape
    n = b.shape[-1]
    tm = min(tm, m)
    return pl.pallas_call(
        functools.partial(_mm_acc_kernel, nk=kdim // tk),
        grid=(m // tm, n // tn, kdim // tk),
        in_specs=[pl.BlockSpec((tm, tk), lambda i, j, k: (i, k)),
                  _weight_spec(b, layer, (tk, tn), lambda i, j, k: (k, j)),
                  pl.BlockSpec((tm, tn), lambda i, j, k: (i, j))],
        out_specs=pl.BlockSpec((tm, tn), lambda i, j, k: (i, j)),
        out_shape=jax.ShapeDtypeStruct((m, n), F32),
        compiler_params=_cparams(("parallel", "parallel", "arbitrary")),
        name=name,
    )(a, b, residual)


def _log_sigmoid(x):
    return jnp.minimum(x, 0.0) - jnp.log1p(jnp.exp(-jnp.abs(x)))


def _sigmoid(x):
    return 1.0 / (1.0 + jnp.exp(-x))


def chunk_tri(t):
    r = lax.broadcasted_iota(jnp.int32, (t, t), 0)
    c = lax.broadcasted_iota(jnp.int32, (t, t), 1)
    same = (r // CHUNK) == (c // CHUNK)
    return jnp.logical_and(same, c <= r).astype(BF16)


def _chunk_cumsum(x, tri):
    hi = x.astype(BF16)
    r1 = x - hi.astype(F32)
    mid = r1.astype(BF16)
    lo = (r1 - mid.astype(F32)).astype(BF16)
    out = jnp.dot(tri, hi, preferred_element_type=F32)
    out += jnp.dot(tri, mid, preferred_element_type=F32)
    out += jnp.dot(tri, lo, preferred_element_type=F32)
    return out


def _split_dot(x, sel):
    hi = x.astype(BF16)
    r1 = x - hi.astype(F32)
    mid = r1.astype(BF16)
    lo = (r1 - mid.astype(F32)).astype(BF16)
    out = jnp.dot(hi, sel, preferred_element_type=F32)
    out += jnp.dot(mid, sel, preferred_element_type=F32)
    out += jnp.dot(lo, sel, preferred_element_type=F32)
    return out


def gate_replicator():
    src = lax.broadcasted_iota(jnp.int32, (LANE, 2 * N_HEADS * LANE), 0)
    grp = lax.broadcasted_iota(jnp.int32, (LANE, 2 * N_HEADS * LANE), 1) // LANE
    want = jnp.where(grp % 2 == 0, SM_MI, SM_MF) + grp // 2
    return (src == want).astype(BF16)


def _dot_nt(a, b):
    return lax.dot_general(a, b, (((1,), (1,)), ((), ())), preferred_element_type=F32)


def _dot_tn(a, b):
    return lax.dot_general(a, b, (((0,), (0,)), ((), ())), preferred_element_type=F32)


def _head_rmsnorm(o, g):
    ms = jnp.mean(o * o, axis=-1, keepdims=True)
    return (o * lax.rsqrt(ms + EPS)) * g


def _gla_kernel(q_ref, k_ref, v_ref, gg_ref, sm_ref, tri_ref, mask_ref, wa_ref, ba_ref, gn_ref,
                o_ref, st_ref, *, t_blk):
    @pl.when(pl.program_id(0) == 0)
    def _():
        st_ref[...] = jnp.zeros_like(st_ref)

    pre = jnp.dot(sm_ref[...].astype(BF16), wa_ref[...],
                  preferred_element_type=F32) + ba_ref[...]
    log_a = _log_sigmoid(pre) / GLA_TAU
    b_all = _chunk_cumsum(log_a, tri_ref[...])
    allowed = mask_ref[...] > 0.5
    scale = QK_DIM ** -0.5
    n_chunks = t_blk // CHUNK
    chunk_rows = [slice(c * CHUNK, (c + 1) * CHUNK) for c in range(n_chunks)]
    heads = range(N_HEADS)
    vcs = [slice(h * V_DIM, (h + 1) * V_DIM) for h in heads]

    q_d, o_intra, d_st, decay = [], [], [], []
    for h in heads:
        kc = slice(h * QK_DIM, (h + 1) * QK_DIM)
        b = b_all[:, kc]
        b_last = [b[r][CHUNK - 1:CHUNK] for r in chunk_rows]
        b_last_rows = jnp.concatenate(
            [jnp.broadcast_to(bl, (CHUNK, QK_DIM)) for bl in b_last], axis=0)
        k = k_ref[:, kc]
        v = v_ref[:, vcs[h]].astype(BF16)
        qd = ((q_ref[:, kc] * scale) * jnp.exp(b)).astype(BF16)
        k_d = (k * jnp.exp(-b)).astype(BF16)
        k_s = (k * jnp.exp(b_last_rows - b)).astype(BF16)
        att = jnp.where(allowed, _dot_nt(qd, k_d), 0.0)
        q_d.append(qd)
        o_intra.append(jnp.dot(att.astype(BF16), v, preferred_element_type=F32))
        d_st.append([_dot_tn(v[r], k_s[r]) for r in chunk_rows])
        decay.append([jnp.exp(bl) for bl in b_last])

    states = [st_ref[h] for h in heads]
    inter = [[] for _ in heads]
    for c, r in enumerate(chunk_rows):
        for h in heads:
            inter[h].append(_dot_nt(q_d[h][r], states[h].astype(BF16)))
            states[h] = states[h] * decay[h][c] + d_st[h][c]
    for h in heads:
        st_ref[h] = states[h]

    for h in heads:
        o = o_intra[h] + jnp.concatenate(inter[h], axis=0)
        gg = gg_ref[:, vcs[h]]
        o = _head_rmsnorm(o, gn_ref[:, vcs[h]]) * (gg * _sigmoid(gg))
        o_ref[:, vcs[h]] = o.astype(o_ref.dtype)


def gla_branch(z, tri, mask, wa_pad, ba, gn, layer, *, t_blk=256):
    s = z.shape[0]
    return pl.pallas_call(
        functools.partial(_gla_kernel, t_blk=t_blk),
        grid=(s // t_blk,),
        in_specs=[
            pl.BlockSpec((t_blk, QK_W), lambda t: (t, _col_block(COL_GQ, QK_W))),
            pl.BlockSpec((t_blk, QK_W), lambda t: (t, _col_block(COL_GK, QK_W))),
            pl.BlockSpec((t_blk, V_W), lambda t: (t, _col_block(COL_GV, V_W))),
            pl.BlockSpec((t_blk, V_W), lambda t: (t, _col_block(COL_GG, V_W))),
            pl.BlockSpec((t_blk, LANE), lambda t: (t, COL_SMALL)),
            pl.BlockSpec((t_blk, t_blk), lambda t: (0, 0)),
            pl.BlockSpec((t_blk, t_blk), lambda t: (0, 0)),
            pl.BlockSpec((None, LANE, QK_W), lambda t: (layer, 0, 0)),
            pl.BlockSpec((None, 1, QK_W), lambda t: (layer, 0, 0)),
            pl.BlockSpec((None, 1, V_W), lambda t: (layer, 0, 0)),
        ],
        out_specs=pl.BlockSpec((t_blk, V_W), lambda t: (t, 0)),
        out_shape=jax.ShapeDtypeStruct((s, V_W), BF16),
        scratch_shapes=[pltpu.VMEM((N_HEADS, V_DIM, QK_DIM), F32)],
        compiler_params=_cparams(("arbitrary",)),
        name="gla",
    )(z, z, z, z, z, tri, mask, wa_pad, ba, gn)


def _mlstm_kernel(qk_ref, v_ref, mo_ref, sm_ref, tri_ref, mask_ref, rep_ref, cw_ref, cb_ref, gb_ref, mn_ref,
                  o_ref, u_ref, c_ref, n_ref, m_ref, *, t_blk):
    @pl.when(pl.program_id(0) == 0)
    def _():
        u_ref[0:SUBLANE, :] = jnp.zeros((SUBLANE, 2 * QK_W), F32)
        c_ref[...] = jnp.zeros_like(c_ref)
        n_ref[...] = jnp.zeros_like(n_ref)
        m_ref[...] = jnp.full(m_ref.shape, -1e30, F32)

    u_ref[SUBLANE:SUBLANE + t_blk, :] = qk_ref[...]
    y = cb_ref[...]
    for j in range(CONV_W):
        off = SUBLANE - (CONV_W - 1) + j
        y = y + cw_ref[j:j + 1, :] * u_ref[off:off + t_blk, :]
    u_ref[0:SUBLANE, :] = u_ref[t_blk:t_blk + SUBLANE, :]
    qk_all = y * _sigmoid(y)

    pre = sm_ref[...] + gb_ref[...]
    f_cum = _chunk_cumsum(_log_sigmoid(pre), tri_ref[...])
    lane = lax.broadcasted_iota(jnp.int32, (t_blk, LANE), 1)
    is_f = jnp.logical_and(lane >= SM_MF, lane < SM_MF + N_HEADS)
    cols = jnp.where(is_f, f_cum, pre)
    rows_t = cols.T

    allowed = mask_ref[...] > 0.5
    n_chunks = t_blk // CHUNK
    chunk_rows = [slice(c * CHUNK, (c + 1) * CHUNK) for c in range(n_chunks)]
    row_chunk = lax.broadcasted_iota(jnp.int32, (t_blk, LANE), 0) // CHUNK
    lane_chunk = lax.broadcasted_iota(jnp.int32, (1, t_blk), 1) // CHUNK
    rep = _split_dot(cols, rep_ref[...])

    def per_chunk_rep(vals):
        out = jnp.broadcast_to(vals[-1], (t_blk, LANE))
        for c in range(n_chunks - 2, -1, -1):
            out = jnp.where(row_chunk == c, vals[c], out)
        return out

    def lanes2(x):
        return jnp.concatenate([x] * (V_DIM // LANE), axis=1)

    k_scale = QK_DIM ** -0.5
    heads = range(N_HEADS)
    vcs = [slice(h * V_DIM, (h + 1) * V_DIM) for h in heads]
    q = [qk_all[:, h * QK_DIM:(h + 1) * QK_DIM] for h in heads]
    k = [qk_all[:, QK_W + h * QK_DIM:QK_W + (h + 1) * QK_DIM] * k_scale for h in heads]
    qb = [x.astype(BF16) for x in q]
    v = [v_ref[:, vc].astype(BF16) for vc in vcs]
    li_rep = [rep[:, (2 * h) * LANE:(2 * h + 1) * LANE] for h in heads]
    f_rep = [rep[:, (2 * h + 1) * LANE:(2 * h + 2) * LANE] for h in heads]
    lir = [rows_t[SM_MI + h:SM_MI + h + 1, :] for h in heads]
    fr = [rows_t[SM_MF + h:SM_MF + h + 1, :] for h in heads]

    f_last, g, d_c, d_n, lmat, l_max, qk = [], [], [], [], [], [], []
    for h in heads:
        fl = [fr[h][:, (c + 1) * CHUNK - 1:(c + 1) * CHUNK] for c in range(n_chunks)]
        fl_row = fl[-1]
        for c in range(n_chunks - 2, -1, -1):
            fl_row = jnp.where(lane_chunk == c, fl[c], fl_row)
        g_row = fl_row - fr[h] + lir[h]
        gh = [jnp.max(jnp.where(lane_chunk == c, g_row, -jnp.inf), axis=-1, keepdims=True)
              for c in range(n_chunks)]
        kw = k[h] * jnp.exp(per_chunk_rep(fl) - f_rep[h] + li_rep[h] - per_chunk_rep(gh))
        kwb = kw.astype(BF16)
        f_last.append(fl)
        g.append(gh)
        d_c.append([_dot_tn(kwb[r], v[h][r]) for r in chunk_rows])
        d_n.append([jnp.sum(kw[r], axis=0, keepdims=True) for r in chunk_rows])
        lm = jnp.where(allowed, lanes2(f_rep[h]) - fr[h] + lir[h], -jnp.inf)
        lmat.append(lm)
        l_max.append(jnp.max(lm, axis=-1, keepdims=True))
        qk.append(_dot_nt(qb[h], k[h].astype(BF16)))

    m_prev = [m_ref[h][:, 0:1] for h in heads]
    c_prev = [c_ref[h] for h in heads]
    n_prev = [n_ref[h] for h in heads]
    m_in = [[] for _ in heads]
    inter = [[] for _ in heads]
    qn = [[] for _ in heads]
    for c, r in enumerate(chunk_rows):
        for h in heads:
            m_in[h].append(m_prev[h])
            inter[h].append(jnp.dot(qb[h][r], c_prev[h].astype(BF16),
                                    preferred_element_type=F32))
            qn[h].append(jnp.sum(q[h][r] * n_prev[h], axis=-1, keepdims=True))
            m_new = jnp.maximum(f_last[h][c] + m_prev[h], g[h][c])
            a = jnp.exp(f_last[h][c] + m_prev[h] - m_new)
            bb = jnp.exp(g[h][c] - m_new)
            c_prev[h] = a * c_prev[h] + bb * d_c[h][c]
            n_prev[h] = a * n_prev[h] + bb * d_n[h][c]
            m_prev[h] = m_new
    for h in heads:
        c_ref[h] = c_prev[h]
        n_ref[h] = n_prev[h]
        m_ref[h] = jnp.broadcast_to(m_prev[h], (1, LANE))

    for h in heads:
        m_inter = f_rep[h] + per_chunk_rep(m_in[h])
        m = jnp.maximum(l_max[h], m_inter)
        s_mat = qk[h] * jnp.exp(lmat[h] - lanes2(m))
        w_inter = jnp.exp(m_inter - m)
        num = jnp.dot(s_mat.astype(BF16), v[h], preferred_element_type=F32)
        num += lanes2(w_inter) * jnp.concatenate(inter[h], axis=0)
        den = jnp.sum(s_mat, axis=-1, keepdims=True)
        den = den + w_inter * jnp.concatenate(qn[h], axis=0)
        hid = num / lanes2(jnp.maximum(jnp.abs(den), jnp.exp(-m)))
        o = _sigmoid(mo_ref[:, vcs[h]]) * hid
        o_ref[:, vcs[h]] = _head_rmsnorm(o, mn_ref[:, vcs[h]]).astype(o_ref.dtype)


def mlstm_branch(z, tri, mask, rep_sel, conv_w, conv_b, gate_bias, mn, layer, *, t_blk=256):
    s = z.shape[0]
    return pl.pallas_call(
        functools.partial(_mlstm_kernel, t_blk=t_blk),
        grid=(s // t_blk,),
        in_specs=[
            pl.BlockSpec((t_blk, 2 * QK_W), lambda t: (t, _col_block(COL_MQK, 2 * QK_W))),
            pl.BlockSpec((t_blk, V_W), lambda t: (t, _col_block(COL_MV, V_W))),
            pl.BlockSpec((t_blk, V_W), lambda t: (t, _col_block(COL_MO, V_W))),
            pl.BlockSpec((t_blk, LANE), lambda t: (t, COL_SMALL)),
            pl.BlockSpec((t_blk, t_blk), lambda t: (0, 0)),
            pl.BlockSpec((t_blk, t_blk), lambda t: (0, 0)),
            pl.BlockSpec((LANE, 2 * N_HEADS * LANE), lambda t: (0, 0)),
            pl.BlockSpec((None, CONV_W, 2 * QK_W), lambda t: (layer, 0, 0)),
            pl.BlockSpec((None, 1, 2 * QK_W), lambda t: (layer, 0, 0)),
            pl.BlockSpec((None, 1, LANE), lambda t: (layer, 0, 0)),
            pl.BlockSpec((None, 1, V_W), lambda t: (layer, 0, 0)),
        ],
        out_specs=pl.BlockSpec((t_blk, V_W), lambda t: (t, 0)),
        out_shape=jax.ShapeDtypeStruct((s, V_W), BF16),
        scratch_shapes=[pltpu.VMEM((SUBLANE + t_blk, 2 * QK_W), F32),
                        pltpu.VMEM((N_HEADS, QK_DIM, V_DIM), F32),
                        pltpu.VMEM((N_HEADS, 1, QK_DIM), F32),
                        pltpu.VMEM((N_HEADS, 1, LANE), F32)],
        compiler_params=_cparams(("arbitrary",)),
        name="mlstm",
    )(z, z, z, z, tri, mask, rep_sel, conv_w, conv_b, gate_bias, mn)


def _memattn_kernel(q_ref, km_ref, vm_ref, o_ref):
    scale = V_DIM ** -0.5
    for h in range(N_HEADS):
        cols = slice(h * V_DIM, (h + 1) * V_DIM)
        q = q_ref[:, cols].astype(BF16)
        s = _dot_nt(q, km_ref[:, cols]) * scale
        e = jnp.exp(s - jnp.max(s, axis=-1, keepdims=True))
        p = e / jnp.sum(e, axis=-1, keepdims=True)
        o = jnp.dot(p.astype(BF16), vm_ref[:, cols], preferred_element_type=F32)
        o_ref[:, cols] = o.astype(o_ref.dtype)


def memattn_branch(z, km, vm, *, tm=512):
    s = z.shape[0]
    tm = min(tm, s)
    return pl.pallas_call(
        _memattn_kernel,
        grid=(s // tm,),
        in_specs=[pl.BlockSpec((tm, V_W), lambda i: (i, _col_block(COL_XQ, V_W))),
                  pl.BlockSpec((MEM_LEN, V_W), lambda i: (0, 0)),
                  pl.BlockSpec((MEM_LEN, V_W), lambda i: (0, 0))],
        out_specs=pl.BlockSpec((tm, V_W), lambda i: (i, 0)),
        out_shape=jax.ShapeDtypeStruct((s, V_W), BF16),
        compiler_params=_cparams(("parallel",)),
        name="memattn",
    )(z, km, vm)


def _merge_kernel(*refs, n_side):
    o1_ref, o2_ref, o3_ref, gl_ref, wb_ref, wg_ref, bg_ref = refs[:7]
    side_in = refs[7:7 + n_side]
    y_ref = refs[7 + n_side]
    side_out = refs[8 + n_side:8 + 2 * n_side]
    wbb_ref, wgb_ref = refs[8 + 2 * n_side:]
    _side_cast(side_in, side_out)

    @pl.when(pl.program_id(1) == 0)
    def _():
        wbb_ref[...] = wb_ref[...].astype(BF16)
        wgb_ref[...] = wg_ref[...].astype(BF16)

    gl = gl_ref[...].astype(BF16)
    y = None
    for j, o_ref in enumerate((o1_ref, o2_ref, o3_ref)):
        gate = _sigmoid(jnp.dot(gl, wgb_ref[j], preferred_element_type=F32)
                        + bg_ref[j:j + 1, :])
        term = gate * jnp.dot(o_ref[...], wbb_ref[j], preferred_element_type=F32)
        y = term if y is None else y + term
    y_ref[...] = y.astype(y_ref.dtype)


def merge_branches(o1, o2, o3, z, wb, wg, bg, layer, *, tm=1024, tn=512, side_cast=()):
    s, w = o1.shape
    d = wb.shape[-1]
    tm = min(tm, s)
    grid = (d // tn, s // tm)
    o_spec = pl.BlockSpec((tm, w), lambda j, i: (i, 0))
    s_in, s_out, s_shapes = _side_cast_specs(side_cast, layer, grid[0] * grid[1],
                                             lambda j, i: j * grid[1] + i)
    return pl.pallas_call(
        functools.partial(_merge_kernel, n_side=len(side_cast)),
        grid=grid,
        in_specs=[o_spec, o_spec, o_spec,
                  pl.BlockSpec((tm, GATE_RANK), lambda j, i: (i, _col_block(COL_GATE, GATE_RANK))),
                  pl.BlockSpec((None, N_BRANCH, w, tn), lambda j, i: (layer, 0, 0, j)),
                  pl.BlockSpec((None, N_BRANCH, GATE_RANK, tn), lambda j, i: (layer, 0, 0, j)),
                  pl.BlockSpec((None, N_BRANCH, tn), lambda j, i: (layer, 0, j))] + s_in,
        out_specs=[pl.BlockSpec((tm, tn), lambda j, i: (i, j))] + s_out,
        out_shape=[jax.ShapeDtypeStruct((s, d), BF16)] + s_shapes,
        scratch_shapes=[pltpu.VMEM((N_BRANCH, w, tn), BF16),
                        pltpu.VMEM((N_BRANCH, GATE_RANK, tn), BF16)],
        compiler_params=_cparams(("parallel", "arbitrary")),
        name="merge",
    )(o1, o2, o3, z, wb, wg, bg, *side_cast)


_IN_GROUPS = ("gq", "gk", "gv", "gg", "ga", "mqk", "mv", "mo", "mi", "mf", "xq", "gate")
_IN_SIZES = (QK_W, QK_W, V_W, V_W, GLA_RANK, 2 * QK_W, V_W, V_W, N_HEADS, N_HEADS, V_W, GATE_RANK)
_IN_SRC = {name: sum(_IN_SIZES[:i]) for i, name in enumerate(_IN_GROUPS)}
_IN_WIDTH = dict(zip(_IN_GROUPS, _IN_SIZES))
_IN_DST = {"gv": COL_GV, "gg": COL_GG, "mv": COL_MV, "mo": COL_MO, "xq": COL_XQ,
           "mqk": COL_MQK, "gq": COL_GQ, "gk": COL_GK, "gate": COL_GATE}
_IN_SMALL = ("ga", "mi", "mf")
assert (SM_GA, SM_MI, SM_MF) == (0, GLA_RANK, GLA_RANK + N_HEADS)


def _permute_wt_kernel(w_ref, o_ref):
    lanes = w_ref.shape[1]
    for name, col in _IN_DST.items():
        src, width = _IN_SRC[name], _IN_WIDTH[name]
        o_ref[col * LANE:col * LANE + width, :] = w_ref[src:src + width, :].astype(o_ref.dtype)
    spans = []
    for n in _IN_SMALL:
        if spans and spans[-1][0] + spans[-1][1] == _IN_SRC[n]:
            spans[-1] = (spans[-1][0], spans[-1][1] + _IN_WIDTH[n])
        else:
            spans.append((_IN_SRC[n], _IN_WIDTH[n]))
    small = [w_ref[src:src + width, :] for src, width in spans]
    used = sum(width for _, width in spans)
    small.append(jnp.zeros(((IN_COLS - COL_SMALL) * LANE - used, lanes), F32))
    o_ref[COL_SMALL * LANE:, :] = jnp.concatenate(small, axis=0).astype(o_ref.dtype)


def permute_w_in_t(w_in_t, *, tc=256):
    depth, n_src, d = w_in_t.shape
    n_dst = IN_COLS * LANE
    return pl.pallas_call(
        _permute_wt_kernel,
        grid=(depth, d // tc),
        in_specs=[pl.BlockSpec((None, n_src, tc), lambda l, i: (l, 0, i))],
        out_specs=pl.BlockSpec((None, n_dst, tc), lambda l, i: (l, 0, i)),
        out_shape=jax.ShapeDtypeStruct((depth, n_dst, d), BF16),
        compiler_params=_cparams(("parallel", "parallel")),
        name="permute_w_in",
    )(w_in_t)


def kernel(x, mem, norm_mix, w_in, w_gla_a, b_gla_a, gla_norm, conv_w, conv_b, b_ml_i, b_ml_f, ml_norm, mem_norm, w_mem_k, w_mem_v, w_branch, w_gate, b_gate, w_out, norm_ffn, w_ff1, w_ff2, final_norm):
    depth = w_in.shape[0]
    _, s, d = x.shape
    xs = x.reshape(s, d)
    t_blk = min(256, s)

    w_in_pt = permute_w_in_t(jnp.swapaxes(w_in, 1, 2))
    wa_pad = jnp.zeros((depth, LANE, QK_W), F32)
    wa_pad = wa_pad.at[:, SM_GA:SM_GA + GLA_RANK, :].set(w_gla_a).astype(BF16)
    gate_bias = jnp.zeros((depth, 1, LANE), F32)
    gate_bias = gate_bias.at[:, 0, SM_MI:SM_MI + N_HEADS].set(b_ml_i)
    gate_bias = gate_bias.at[:, 0, SM_MF:SM_MF + N_HEADS].set(b_ml_f)
    tri = chunk_tri(t_blk)
    tri_f = tri.astype(F32)
    rep_sel = gate_replicator()
    b_gla_a3 = b_gla_a.reshape(depth, 1, QK_W)
    gla_norm3 = gla_norm.reshape(depth, 1, V_W)
    conv_b3 = conv_b.reshape(depth, 1, 2 * QK_W)
    ml_norm3 = ml_norm.reshape(depth, 1, V_W)

    memn = rmsnorm(mem.reshape(MEM_LEN, d), mem_norm, BF16, tm=MEM_LEN)

    for l in range(depth):
        z, = norm_matmul_rows(xs, norm_mix[l], w_in_pt, l, tm=1024, tn=768, out_dtype=F32,
                              transposed_b=True, name="in_proj")
        km = matmul_cols(memn, w_mem_k, l, tm=MEM_LEN, tn=256, out_dtype=BF16, name="mem_k")
        vm = matmul_cols(memn, w_mem_v, l, tm=MEM_LEN, tn=256, out_dtype=BF16, name="mem_v")
        o_gla = gla_branch(z, tri, tri_f, wa_pad, b_gla_a3, gla_norm3, l, t_blk=t_blk)
        o_ml = mlstm_branch(z, tri, tri_f, rep_sel, conv_w, conv_b3, gate_bias, ml_norm3, l, t_blk=t_blk)
        o_mem = memattn_branch(z, km, vm)
        y, w_out_b, w_ff1_b = merge_branches(o_gla, o_ml, o_mem, z, w_branch, w_gate, b_gate, l,
                                             side_cast=(w_out, w_ff1))
        xs = matmul_acc(y, w_out_b, l, xs, tm=1024, tn=1024, tk=d, name="out_proj")
        hid, w_ff2_b = norm_matmul_rows(xs, norm_ffn[l], w_ff1_b, l, tm=1024, tn=1024,
                                        out_dtype=BF16, epilogue="relu2", side_cast=(w_ff2,),
                                        name="ffn_up")
        xs = matmul_acc(hid, w_ff2_b, l, xs, tm=1024, tn=1024, tk=4096, name="ffn_down")
    return rmsnorm(xs, final_norm, F32).reshape(x.shape)
```

```python
import functools

import jax
import jax.numpy as jnp
from jax import lax
from jax.experimental import pallas as pl
from jax.experimental.pallas import tpu as pltpu

F32 = jnp.float32
BF16 = jnp.bfloat16

EPS = 1e-6
CHUNK = 64
N_HEADS = 4
QK_DIM = 128
V_DIM = 256
QK_W = N_HEADS * QK_DIM
V_W = N_HEADS * V_DIM
GLA_RANK = 16
GLA_TAU = 16.0
CONV_W = 4
MEM_LEN = 256
GATE_RANK = 256
N_BRANCH = 3

LANE = 128
SUBLANE = 8
VMEM_LIMIT = 56 * 1024 * 1024

TM = 1024
TN_IN, TN_MERGE, TN_OUT, TN_FFN = 768, 512, 1024, 1024
TK_FFN_DOWN = 4096
T_MIX = 4 * CHUNK
TM_ATTN = 1024
TM_NORM = 512
TN_MEM = 256

COL_GV, COL_GG, COL_MV, COL_MO, COL_XQ = 0, 8, 16, 24, 32
COL_MQK, COL_GQ, COL_GK = 40, 48, 52
COL_GATE, COL_SMALL, IN_COLS = 56, 58, 60
SM_GA, SM_MI, SM_MF = 0, 16, 20


def _cparams(sem):
    return pltpu.CompilerParams(dimension_semantics=sem, vmem_limit_bytes=VMEM_LIMIT)


def _col_block(col, width):
    return col * LANE // width


def _rmsnorm_kernel(x_ref, g_ref, o_ref):
    x = x_ref[...]
    ms = jnp.mean(x * x, axis=-1, keepdims=True)
    o_ref[...] = ((x * lax.rsqrt(ms + EPS)) * g_ref[...]).astype(o_ref.dtype)


def rmsnorm(x, g, out_dtype, tm=TM_NORM):
    m, d = x.shape
    tm = min(tm, m)
    return pl.pallas_call(
        _rmsnorm_kernel,
        grid=(m // tm,),
        in_specs=[pl.BlockSpec((tm, d), lambda i: (i, 0)),
                  pl.BlockSpec((1, d), lambda i: (0, 0))],
        out_specs=pl.BlockSpec((tm, d), lambda i: (i, 0)),
        out_shape=jax.ShapeDtypeStruct((m, d), out_dtype),
        compiler_params=_cparams(("parallel",)),
        name="rmsnorm",
    )(x, g.reshape(1, d))


def _apply_epilogue(acc, epilogue):
    if epilogue == "relu2":
        return jnp.square(jnp.maximum(acc, 0.0))
    return acc


def _side_cast_specs(srcs, layer, n_steps, step_of):
    in_specs, out_specs, out_shapes = [], [], []
    for w in srcs:
        r, c = w.shape[-2:]
        rows = r // n_steps
        in_specs.append(pl.BlockSpec((None, rows, c), lambda *g: (layer, step_of(*g), 0)))
        out_specs.append(pl.BlockSpec((rows, c), lambda *g: (step_of(*g), 0)))
        out_shapes.append(jax.ShapeDtypeStruct((r, c), BF16))
    return in_specs, out_specs, out_shapes


def _side_cast(side_in, side_out):
    for src_ref, dst_ref in zip(side_in, side_out):
        dst_ref[...] = src_ref[...].astype(dst_ref.dtype)


def _weight_spec(b, layer, block, index):
    if b.ndim == 2:
        return pl.BlockSpec(block, index)
    return pl.BlockSpec((None,) + block, lambda *g: (layer,) + index(*g))


def _norm_mm_rows_kernel(*refs, transposed_b, epilogue, n_side, ts, n_slab):
    x_ref, g_ref, b_ref = refs[:3]
    side_in = refs[3:3 + n_side]
    o_ref = refs[3 + n_side]
    side_out = refs[4 + n_side:4 + 2 * n_side]
    h_even, h_odd = refs[4 + 2 * n_side:]
    r = pl.program_id(0)
    slab = jnp.minimum(pl.program_id(1), n_slab - 1)
    dims = (((1,), (1 if transposed_b else 0,)), ((), ()))

    def step(read_ref, write_ref):
        if read_ref is not None:
            acc = lax.dot_general(read_ref[...], b_ref[...], dims, preferred_element_type=F32)
            o_ref[...] = _apply_epilogue(acc, epilogue).astype(o_ref.dtype)
        x = x_ref[...]
        ms = jnp.mean(x * x, axis=-1, keepdims=True)
        h = ((x * lax.rsqrt(ms + EPS)) * g_ref[...]).astype(write_ref.dtype)
        write_ref[pl.ds(pl.multiple_of(slab * ts, ts), ts), :] = h
        _side_cast(side_in, side_out)

    @pl.when(r == 0)
    def _():
        step(None, h_even)

    @pl.when(r % 2 == 1)
    def _():
        step(h_even, h_odd)

    @pl.when(jnp.logical_and(r > 0, r % 2 == 0))
    def _():
        step(h_odd, h_even)


def norm_matmul_rows(x, g, b, layer, *, tm, tn, out_dtype, name, epilogue=None,
                     transposed_b=False, side_cast=()):
    m, kdim = x.shape
    n = b.shape[-2] if transposed_b else b.shape[-1]
    tm = min(tm, m)
    ni, nj = m // tm, n // tn
    n_slab = 1 << (nj.bit_length() - 1)
    ts = tm // n_slab
    def col(r, j):
        return jnp.where(r == 0, 0, j)

    b_spec = (_weight_spec(b, layer, (tn, kdim), lambda r, j: (col(r, j), 0)) if transposed_b
              else _weight_spec(b, layer, (kdim, tn), lambda r, j: (0, col(r, j))))
    s_in, s_out, s_shapes = _side_cast_specs(
        side_cast, layer, ni * nj, lambda r, j: jnp.where(r == 0, 0, (r - 1) * nj + j))
    x_spec = pl.BlockSpec(
        (ts, kdim),
        lambda r, j: (jnp.minimum(r, ni - 1) * n_slab + jnp.minimum(j, n_slab - 1), 0))
    return pl.pallas_call(
        functools.partial(_norm_mm_rows_kernel, transposed_b=transposed_b, epilogue=epilogue,
                          n_side=len(side_cast), ts=ts, n_slab=n_slab),
        grid=(ni + 1, nj),
        in_specs=[x_spec, pl.BlockSpec((1, kdim), lambda r, j: (0, 0)), b_spec] + s_in,
        out_specs=[pl.BlockSpec((tm, tn), lambda r, j: (jnp.maximum(r - 1, 0), col(r, j)))] + s_out,
        out_shape=[jax.ShapeDtypeStruct((m, n), out_dtype)] + s_shapes,
        scratch_shapes=[pltpu.VMEM((tm, kdim), BF16), pltpu.VMEM((tm, kdim), BF16)],
        compiler_params=_cparams(("arbitrary", "arbitrary")),
        name=name,
    )(x, g.reshape(1, kdim), b, *side_cast)


def _mm_cols_kernel(a_ref, b_ref, o_ref, wb_ref, *, epilogue):
    @pl.when(pl.program_id(1) == 0)
    def _():
        wb_ref[...] = b_ref[...].astype(BF16)

    acc = jnp.dot(a_ref[...].astype(BF16), wb_ref[...], preferred_element_type=F32)
    o_ref[...] = _apply_epilogue(acc, epilogue).astype(o_ref.dtype)


def matmul_cols(a, b, layer, *, tm, tn, out_dtype, epilogue=None, name):
    m, kdim = a.shape
    n = b.shape[-1]
    tm = min(tm, m)
    return pl.pallas_call(
        functools.partial(_mm_cols_kernel, epilogue=epilogue),
        grid=(n // tn, m // tm),
        in_specs=[pl.BlockSpec((tm, kdim), lambda j, i: (i, 0)),
                  pl.BlockSpec((None, kdim, tn), lambda j, i: (layer, 0, j))],
        out_specs=pl.BlockSpec((tm, tn), lambda j, i: (i, j)),
        out_shape=jax.ShapeDtypeStruct((m, n), out_dtype),
        scratch_shapes=[pltpu.VMEM((kdim, tn), BF16)],
        compiler_params=_cparams(("parallel", "arbitrary")),
        name=name,
    )(a, b)


def _mm_acc_kernel(a_ref, b_ref, r_ref, o_ref, *, nk):
    def partial_sum():
        return jnp.dot(a_ref[...], b_ref[...], preferred_element_type=F32)

    if nk == 1:
        o_ref[...] = partial_sum() + r_ref[...]
        return
    k = pl.program_id(2)

    @pl.when(k == 0)
    def _():
        o_ref[...] = partial_sum()

    @pl.when(jnp.logical_and(k > 0, k < nk - 1))
    def _():
        o_ref[...] = partial_sum() + o_ref[...]

    @pl.when(k == nk - 1)
    def _():
        o_ref[...] = (partial_sum() + o_ref[...]) + r_ref[...]


def matmul_acc(a, b, layer, residual, *, tm, tn, tk, name):
    m, kdim = a.shape
    n = b.shape[-1]
    tm = min(tm, m)
    return pl.pallas_call(
        functools.partial(_mm_acc_kernel, nk=kdim // tk),
        grid=(m // tm, n // tn, kdim // tk),
        in_specs=[pl.BlockSpec((tm, tk), lambda i, j, k: (i, k)),
                  _weight_spec(b, layer, (tk, tn), lambda i, j, k: (k, j)),
                  pl.BlockSpec((tm, tn), lambda i, j, k: (i, j))],
        out_specs=pl.BlockSpec((tm, tn), lambda i, j, k: (i, j)),
        out_shape=jax.ShapeDtypeStruct((m, n), F32),
        compiler_params=_cparams(("parallel", "parallel", "arbitrary")),
        name=name,
    )(a, b, residual)


def _log_sigmoid(x):
    return jnp.minimum(x, 0.0) - jnp.log1p(jnp.exp(-jnp.abs(x)))


def _sigmoid(x):
    return 1.0 / (1.0 + jnp.exp(-x))


def chunk_tri(t):
    r = lax.broadcasted_iota(jnp.int32, (t, t), 0)
    c = lax.broadcasted_iota(jnp.int32, (t, t), 1)
    same = (r // CHUNK) == (c // CHUNK)
    return jnp.logical_and(same, c <= r).astype(BF16)


def _chunk_cumsum(x, tri):
    hi = x.astype(BF16)
    r1 = x - hi.astype(F32)
    mid = r1.astype(BF16)
    lo = (r1 - mid.astype(F32)).astype(BF16)
    out = jnp.dot(tri, hi, preferred_element_type=F32)
    out += jnp.dot(tri, mid, preferred_element_type=F32)
    out += jnp.dot(tri, lo, preferred_element_type=F32)
    return out


def _split_dot(x, sel):
    hi = x.astype(BF16)
    r1 = x - hi.astype(F32)
    mid = r1.astype(BF16)
    lo = (r1 - mid.astype(F32)).astype(BF16)
    out = jnp.dot(hi, sel, preferred_element_type=F32)
    out += jnp.dot(mid, sel, preferred_element_type=F32)
    out += jnp.dot(lo, sel, preferred_element_type=F32)
    return out


def gate_replicator():
    src = lax.broadcasted_iota(jnp.int32, (LANE, 2 * N_HEADS * LANE), 0)
    grp = lax.broadcasted_iota(jnp.int32, (LANE, 2 * N_HEADS * LANE), 1) // LANE
    want = jnp.where(grp % 2 == 0, SM_MI, SM_MF) + grp // 2
    return (src == want).astype(BF16)


def _dot_nt(a, b):
    return lax.dot_general(a, b, (((1,), (1,)), ((), ())), preferred_element_type=F32)


def _dot_tn(a, b):
    return lax.dot_general(a, b, (((0,), (0,)), ((), ())), preferred_element_type=F32)


def _head_rmsnorm(o, g):
    ms = jnp.mean(o * o, axis=-1, keepdims=True)
    return (o * lax.rsqrt(ms + EPS)) * g


def _gla_kernel(q_ref, k_ref, v_ref, gg_ref, sm_ref, tri_ref, mask_ref, wa_ref, ba_ref, gn_ref,
                o_ref, st_ref, *, t_blk):
    @pl.when(pl.program_id(0) == 0)
    def _():
        st_ref[...] = jnp.zeros_like(st_ref)

    pre = jnp.dot(sm_ref[...].astype(BF16), wa_ref[...],
                  preferred_element_type=F32) + ba_ref[...]
    log_a = _log_sigmoid(pre) / GLA_TAU
    b_all = _chunk_cumsum(log_a, tri_ref[...])
    allowed = mask_ref[...] > 0.5
    scale = QK_DIM ** -0.5
    n_chunks = t_blk // CHUNK
    chunk_rows = [slice(c * CHUNK, (c + 1) * CHUNK) for c in range(n_chunks)]
    heads = range(N_HEADS)
    vcs = [slice(h * V_DIM, (h + 1) * V_DIM) for h in heads]

    q_d, o_intra, d_st, decay = [], [], [], []
    for h in heads:
        kc = slice(h * QK_DIM, (h + 1) * QK_DIM)
        b = b_all[:, kc]
        b_last = [b[r][CHUNK - 1:CHUNK] for r in chunk_rows]
        b_last_rows = jnp.concatenate(
            [jnp.broadcast_to(bl, (CHUNK, QK_DIM)) for bl in b_last], axis=0)
        k = k_ref[:, kc]
        v = v_ref[:, vcs[h]].astype(BF16)
        qd = ((q_ref[:, kc] * scale) * jnp.exp(b)).astype(BF16)
        k_d = (k * jnp.exp(-b)).astype(BF16)
        k_s = (k * jnp.exp(b_last_rows - b)).astype(BF16)
        att = jnp.where(allowed, _dot_nt(qd, k_d), 0.0)
        q_d.append(qd)
        o_intra.append(jnp.dot(att.astype(BF16), v, preferred_element_type=F32))
        d_st.append([_dot_tn(v[r], k_s[r]) for r in chunk_rows])
        decay.append([jnp.exp(bl) for bl in b_last])

    states = [st_ref[h] for h in heads]
    inter = [[] for _ in heads]
    for c, r in enumerate(chunk_rows):
        for h in heads:
            inter[h].append(_dot_nt(q_d[h][r], states[h].astype(BF16)))
            states[h] = states[h] * decay[h][c] + d_st[h][c]
    for h in heads:
        st_ref[h] = states[h]

    for h in heads:
        o = o_intra[h] + jnp.concatenate(inter[h], axis=0)
        gg = gg_ref[:, vcs[h]]
        o = _head_rmsnorm(o, gn_ref[:, vcs[h]]) * (gg * _sigmoid(gg))
        o_ref[:, vcs[h]] = o.astype(o_ref.dtype)


def gla_branch(z, tri, mask, wa_pad, ba, gn, layer, *, t_blk):
    s = z.shape[0]
    return pl.pallas_call(
        functools.partial(_gla_kernel, t_blk=t_blk),
        grid=(s // t_blk,),
        in_specs=[
            pl.BlockSpec((t_blk, QK_W), lambda t: (t, _col_block(COL_GQ, QK_W))),
            pl.BlockSpec((t_blk, QK_W), lambda t: (t, _col_block(COL_GK, QK_W))),
            pl.BlockSpec((t_blk, V_W), lambda t: (t, _col_block(COL_GV, V_W))),
            pl.BlockSpec((t_blk, V_W), lambda t: (t, _col_block(COL_GG, V_W))),
            pl.BlockSpec((t_blk, LANE), lambda t: (t, COL_SMALL)),
            pl.BlockSpec((t_blk, t_blk), lambda t: (0, 0)),
            pl.BlockSpec((t_blk, t_blk), lambda t: (0, 0)),
            pl.BlockSpec((None, LANE, QK_W), lambda t: (layer, 0, 0)),
            pl.BlockSpec((None, 1, QK_W), lambda t: (layer, 0, 0)),
            pl.BlockSpec((None, 1, V_W), lambda t: (layer, 0, 0)),
        ],
        out_specs=pl.BlockSpec((t_blk, V_W), lambda t: (t, 0)),
        out_shape=jax.ShapeDtypeStruct((s, V_W), BF16),
        scratch_shapes=[pltpu.VMEM((N_HEADS, V_DIM, QK_DIM), F32)],
        compiler_params=_cparams(("arbitrary",)),
        name="gla",
    )(z, z, z, z, z, tri, mask, wa_pad, ba, gn)


def _mlstm_kernel(qk_ref, v_ref, mo_ref, sm_ref, tri_ref, mask_ref, rep_ref, cw_ref, cb_ref, gb_ref, mn_ref,
                  o_ref, u_ref, c_ref, n_ref, m_ref, *, t_blk):
    @pl.when(pl.program_id(0) == 0)
    def _():
        u_ref[0:SUBLANE, :] = jnp.zeros((SUBLANE, 2 * QK_W), F32)
        c_ref[...] = jnp.zeros_like(c_ref)
        n_ref[...] = jnp.zeros_like(n_ref)
        m_ref[...] = jnp.full(m_ref.shape, -1e30, F32)

    u_ref[SUBLANE:SUBLANE + t_blk, :] = qk_ref[...]
    y = cb_ref[...]
    for j in range(CONV_W):
        off = SUBLANE - (CONV_W - 1) + j
        y = y + cw_ref[j:j + 1, :] * u_ref[off:off + t_blk, :]
    u_ref[0:SUBLANE, :] = u_ref[t_blk:t_blk + SUBLANE, :]
    qk_all = y * _sigmoid(y)

    pre = sm_ref[...] + gb_ref[...]
    f_cum = _chunk_cumsum(_log_sigmoid(pre), tri_ref[...])
    lane = lax.broadcasted_iota(jnp.int32, (t_blk, LANE), 1)
    is_f = jnp.logical_and(lane >= SM_MF, lane < SM_MF + N_HEADS)
    cols = jnp.where(is_f, f_cum, pre)
    rows_t = cols.T

    allowed = mask_ref[...] > 0.5
    n_chunks = t_blk // CHUNK
    chunk_rows = [slice(c * CHUNK, (c + 1) * CHUNK) for c in range(n_chunks)]
    row_chunk = lax.broadcasted_iota(jnp.int32, (t_blk, LANE), 0) // CHUNK
    lane_chunk = lax.broadcasted_iota(jnp.int32, (1, t_blk), 1) // CHUNK
    rep = _split_dot(cols, rep_ref[...])

    def per_chunk_rep(vals):
        out = jnp.broadcast_to(vals[-1], (t_blk, LANE))
        for c in range(n_chunks - 2, -1, -1):
            out = jnp.where(row_chunk == c, vals[c], out)
        return out

    def lanes2(x):
        return jnp.concatenate([x] * (V_DIM // LANE), axis=1)

    k_scale = QK_DIM ** -0.5
    heads = range(N_HEADS)
    vcs = [slice(h * V_DIM, (h + 1) * V_DIM) for h in heads]
    q = [qk_all[:, h * QK_DIM:(h + 1) * QK_DIM] for h in heads]
    k = [qk_all[:, QK_W + h * QK_DIM:QK_W + (h + 1) * QK_DIM] * k_scale for h in heads]
    qb = [x.astype(BF16) for x in q]
    v = [v_ref[:, vc].astype(BF16) for vc in vcs]
    li_rep = [rep[:, (2 * h) * LANE:(2 * h + 1) * LANE] for h in heads]
    f_rep = [rep[:, (2 * h + 1) * LANE:(2 * h + 2) * LANE] for h in heads]
    lir = [rows_t[SM_MI + h:SM_MI + h + 1, :] for h in heads]
    fr = [rows_t[SM_MF + h:SM_MF + h + 1, :] for h in heads]

    f_last, g, d_c, d_n, lmat, l_max, qk = [], [], [], [], [], [], []
    for h in heads:
        fl = [fr[h][:, (c + 1) * CHUNK - 1:(c + 1) * CHUNK] for c in range(n_chunks)]
        fl_row = fl[-1]
        for c in range(n_chunks - 2, -1, -1):
            fl_row = jnp.where(lane_chunk == c, fl[c], fl_row)
        g_row = fl_row - fr[h] + lir[h]
        gh = [jnp.max(jnp.where(lane_chunk == c, g_row, -jnp.inf), axis=-1, keepdims=True)
              for c in range(n_chunks)]
        kw = k[h] * jnp.exp(per_chunk_rep(fl) - f_rep[h] + li_rep[h] - per_chunk_rep(gh))
        kwb = kw.astype(BF16)
        f_last.append(fl)
        g.append(gh)
        d_c.append([_dot_tn(kwb[r], v[h][r]) for r in chunk_rows])
        d_n.append([jnp.sum(kw[r], axis=0, keepdims=True) for r in chunk_rows])
        lm = jnp.where(allowed, lanes2(f_rep[h]) - fr[h] + lir[h], -jnp.inf)
        lmat.append(lm)
        l_max.append(jnp.max(lm, axis=-1, keepdims=True))
        qk.append(_dot_nt(qb[h], k[h].astype(BF16)))

    m_prev = [m_ref[h][:, 0:1] for h in heads]
    c_prev = [c_ref[h] for h in heads]
    n_prev = [n_ref[h] for h in heads]
    m_in = [[] for _ in heads]
    inter = [[] for _ in heads]
    qn = [[] for _ in heads]
    for c, r in enumerate(chunk_rows):
        for h in heads:
            m_in[h].append(m_prev[h])
            inter[h].append(jnp.dot(qb[h][r], c_prev[h].astype(BF16),
                                    preferred_element_type=F32))
            qn[h].append(jnp.sum(q[h][r] * n_prev[h], axis=-1, keepdims=True))
            m_new = jnp.maximum(f_last[h][c] + m_prev[h], g[h][c])
            a = jnp.exp(f_last[h][c] + m_prev[h] - m_new)
            bb = jnp.exp(g[h][c] - m_new)
            c_prev[h] = a * c_prev[h] + bb * d_c[h][c]
            n_prev[h] = a * n_prev[h] + bb * d_n[h][c]
            m_prev[h] = m_new
    for h in heads:
        c_ref[h] = c_prev[h]
        n_ref[h] = n_prev[h]
        m_ref[h] = jnp.broadcast_to(m_prev[h], (1, LANE))

    for h in heads:
        m_inter = f_rep[h] + per_chunk_rep(m_in[h])
        m = jnp.maximum(l_max[h], m_inter)
        s_mat = qk[h] * jnp.exp(lmat[h] - lanes2(m))
        w_inter = jnp.exp(m_inter - m)
        num = jnp.dot(s_mat.astype(BF16), v[h], preferred_element_type=F32)
        num += lanes2(w_inter) * jnp.concatenate(inter[h], axis=0)
        den = jnp.sum(s_mat, axis=-1, keepdims=True)
        den = den + w_inter * jnp.concatenate(qn[h], axis=0)
        hid = num / lanes2(jnp.maximum(jnp.abs(den), jnp.exp(-m)))
        o = _sigmoid(mo_ref[:, vcs[h]]) * hid
        o_ref[:, vcs[h]] = _head_rmsnorm(o, mn_ref[:, vcs[h]]).astype(o_ref.dtype)


def mlstm_branch(z, tri, mask, rep_sel, conv_w, conv_b, gate_bias, mn, layer, *, t_blk):
    s = z.shape[0]
    return pl.pallas_call(
        functools.partial(_mlstm_kernel, t_blk=t_blk),
        grid=(s // t_blk,),
        in_specs=[
            pl.BlockSpec((t_blk, 2 * QK_W), lambda t: (t, _col_block(COL_MQK, 2 * QK_W))),
            pl.BlockSpec((t_blk, V_W), lambda t: (t, _col_block(COL_MV, V_W))),
            pl.BlockSpec((t_blk, V_W), lambda t: (t, _col_block(COL_MO, V_W))),
            pl.BlockSpec((t_blk, LANE), lambda t: (t, COL_SMALL)),
            pl.BlockSpec((t_blk, t_blk), lambda t: (0, 0)),
            pl.BlockSpec((t_blk, t_blk), lambda t: (0, 0)),
            pl.BlockSpec((LANE, 2 * N_HEADS * LANE), lambda t: (0, 0)),
            pl.BlockSpec((None, CONV_W, 2 * QK_W), lambda t: (layer, 0, 0)),
            pl.BlockSpec((None, 1, 2 * QK_W), lambda t: (layer, 0, 0)),
            pl.BlockSpec((None, 1, LANE), lambda t: (layer, 0, 0)),
            pl.BlockSpec((None, 1, V_W), lambda t: (layer, 0, 0)),
        ],
        out_specs=pl.BlockSpec((t_blk, V_W), lambda t: (t, 0)),
        out_shape=jax.ShapeDtypeStruct((s, V_W), BF16),
        scratch_shapes=[pltpu.VMEM((SUBLANE + t_blk, 2 * QK_W), F32),
                        pltpu.VMEM((N_HEADS, QK_DIM, V_DIM), F32),
                        pltpu.VMEM((N_HEADS, 1, QK_DIM), F32),
                        pltpu.VMEM((N_HEADS, 1, LANE), F32)],
        compiler_params=_cparams(("arbitrary",)),
        name="mlstm",
    )(z, z, z, z, tri, mask, rep_sel, conv_w, conv_b, gate_bias, mn)


def _memattn_kernel(q_ref, km_ref, vm_ref, o_ref):
    scale = V_DIM ** -0.5
    for h in range(N_HEADS):
        cols = slice(h * V_DIM, (h + 1) * V_DIM)
        q = q_ref[:, cols].astype(BF16)
        s = _dot_nt(q, km_ref[:, cols]) * scale
        e = jnp.exp(s - jnp.max(s, axis=-1, keepdims=True))
        p = e / jnp.sum(e, axis=-1, keepdims=True)
        o = jnp.dot(p.astype(BF16), vm_ref[:, cols], preferred_element_type=F32)
        o_ref[:, cols] = o.astype(o_ref.dtype)


def memattn_branch(z, km, vm, *, tm=TM_ATTN):
    s = z.shape[0]
    tm = min(tm, s)
    return pl.pallas_call(
        _memattn_kernel,
        grid=(s // tm,),
        in_specs=[pl.BlockSpec((tm, V_W), lambda i: (i, _col_block(COL_XQ, V_W))),
                  pl.BlockSpec((MEM_LEN, V_W), lambda i: (0, 0)),
                  pl.BlockSpec((MEM_LEN, V_W), lambda i: (0, 0))],
        out_specs=pl.BlockSpec((tm, V_W), lambda i: (i, 0)),
        out_shape=jax.ShapeDtypeStruct((s, V_W), BF16),
        compiler_params=_cparams(("parallel",)),
        name="memattn",
    )(z, km, vm)


def _merge_kernel(*refs, n_side):
    o1_ref, o2_ref, o3_ref, gl_ref, wb_ref, wg_ref, bg_ref = refs[:7]
    side_in = refs[7:7 + n_side]
    y_ref = refs[7 + n_side]
    side_out = refs[8 + n_side:8 + 2 * n_side]
    wbb_ref, wgb_ref = refs[8 + 2 * n_side:]
    _side_cast(side_in, side_out)

    @pl.when(pl.program_id(1) == 0)
    def _():
        wbb_ref[...] = wb_ref[...].astype(BF16)
        wgb_ref[...] = wg_ref[...].astype(BF16)

    gl = gl_ref[...].astype(BF16)
    y = None
    for j, o_ref in enumerate((o1_ref, o2_ref, o3_ref)):
        gate = _sigmoid(jnp.dot(gl, wgb_ref[j], preferred_element_type=F32)
                        + bg_ref[j:j + 1, :])
        term = gate * jnp.dot(o_ref[...], wbb_ref[j], preferred_element_type=F32)
        y = term if y is None else y + term
    y_ref[...] = y.astype(y_ref.dtype)


def merge_branches(o1, o2, o3, z, wb, wg, bg, layer, *, tm=TM, tn=TN_MERGE, side_cast=()):
    s, w = o1.shape
    d = wb.shape[-1]
    tm = min(tm, s)
    grid = (d // tn, s // tm)
    o_spec = pl.BlockSpec((tm, w), lambda j, i: (i, 0))
    s_in, s_out, s_shapes = _side_cast_specs(side_cast, layer, grid[0] * grid[1],
                                             lambda j, i: j * grid[1] + i)
    return pl.pallas_call(
        functools.partial(_merge_kernel, n_side=len(side_cast)),
        grid=grid,
        in_specs=[o_spec, o_spec, o_spec,
                  pl.BlockSpec((tm, GATE_RANK), lambda j, i: (i, _col_block(COL_GATE, GATE_RANK))),
                  pl.BlockSpec((None, N_BRANCH, w, tn), lambda j, i: (layer, 0, 0, j)),
                  pl.BlockSpec((None, N_BRANCH, GATE_RANK, tn), lambda j, i: (layer, 0, 0, j)),
                  pl.BlockSpec((None, N_BRANCH, tn), lambda j, i: (layer, 0, j))] + s_in,
        out_specs=[pl.BlockSpec((tm, tn), lambda j, i: (i, j))] + s_out,
        out_shape=[jax.ShapeDtypeStruct((s, d), BF16)] + s_shapes,
        scratch_shapes=[pltpu.VMEM((N_BRANCH, w, tn), BF16),
                        pltpu.VMEM((N_BRANCH, GATE_RANK, tn), BF16)],
        compiler_params=_cparams(("parallel", "arbitrary")),
        name="merge",
    )(o1, o2, o3, z, wb, wg, bg, *side_cast)


_IN_GROUPS = ("gq", "gk", "gv", "gg", "ga", "mqk", "mv", "mo", "mi", "mf", "xq", "gate")
_IN_SIZES = (QK_W, QK_W, V_W, V_W, GLA_RANK, 2 * QK_W, V_W, V_W, N_HEADS, N_HEADS, V_W, GATE_RANK)
_IN_SRC = {name: sum(_IN_SIZES[:i]) for i, name in enumerate(_IN_GROUPS)}
_IN_WIDTH = dict(zip(_IN_GROUPS, _IN_SIZES))
_IN_DST = {"gv": COL_GV, "gg": COL_GG, "mv": COL_MV, "mo": COL_MO, "xq": COL_XQ,
           "mqk": COL_MQK, "gq": COL_GQ, "gk": COL_GK, "gate": COL_GATE}
_IN_SMALL = ("ga", "mi", "mf")
assert (SM_GA, SM_MI, SM_MF) == (0, GLA_RANK, GLA_RANK + N_HEADS)


def _permute_wt_kernel(w_ref, o_ref):
    lanes = w_ref.shape[1]
    for name, col in _IN_DST.items():
        src, width = _IN_SRC[name], _IN_WIDTH[name]
        o_ref[col * LANE:col * LANE + width, :] = w_ref[src:src + width, :].astype(o_ref.dtype)
    spans = []
    for n in _IN_SMALL:
        if spans and spans[-1][0] + spans[-1][1] == _IN_SRC[n]:
            spans[-1] = (spans[-1][0], spans[-1][1] + _IN_WIDTH[n])
        else:
            spans.append((_IN_SRC[n], _IN_WIDTH[n]))
    small = [w_ref[src:src + width, :] for src, width in spans]
    used = sum(width for _, width in spans)
    small.append(jnp.zeros(((IN_COLS - COL_SMALL) * LANE - used, lanes), F32))
    o_ref[COL_SMALL * LANE:, :] = jnp.concatenate(small, axis=0).astype(o_ref.dtype)


def permute_w_in_t(w_in_t, *, tc=256):
    depth, n_src, d = w_in_t.shape
    n_dst = IN_COLS * LANE
    return pl.pallas_call(
        _permute_wt_kernel,
        grid=(depth, d // tc),
        in_specs=[pl.BlockSpec((None, n_src, tc), lambda l, i: (l, 0, i))],
        out_specs=pl.BlockSpec((None, n_dst, tc), lambda l, i: (l, 0, i)),
        out_shape=jax.ShapeDtypeStruct((depth, n_dst, d), BF16),
        compiler_params=_cparams(("parallel", "parallel")),
        name="permute_w_in",
    )(w_in_t)


def kernel(x, mem, norm_mix, w_in, w_gla_a, b_gla_a, gla_norm, conv_w, conv_b, b_ml_i, b_ml_f, ml_norm, mem_norm, w_mem_k, w_mem_v, w_branch, w_gate, b_gate, w_out, norm_ffn, w_ff1, w_ff2, final_norm):
    depth = w_in.shape[0]
    _, s, d = x.shape
    xs = x.reshape(s, d)
    t_blk = min(T_MIX, s)

    w_in_pt = permute_w_in_t(jnp.swapaxes(w_in, 1, 2))
    wa_pad = jnp.zeros((depth, LANE, QK_W), F32)
    wa_pad = wa_pad.at[:, SM_GA:SM_GA + GLA_RANK, :].set(w_gla_a).astype(BF16)
    gate_bias = jnp.zeros((depth, 1, LANE), F32)
    gate_bias = gate_bias.at[:, 0, SM_MI:SM_MI + N_HEADS].set(b_ml_i)
    gate_bias = gate_bias.at[:, 0, SM_MF:SM_MF + N_HEADS].set(b_ml_f)
    tri = chunk_tri(t_blk)
    tri_f = tri.astype(F32)
    rep_sel = gate_replicator()
    b_gla_a3 = b_gla_a.reshape(depth, 1, QK_W)
    gla_norm3 = gla_norm.reshape(depth, 1, V_W)
    conv_b3 = conv_b.reshape(depth, 1, 2 * QK_W)
    ml_norm3 = ml_norm.reshape(depth, 1, V_W)

    memn = rmsnorm(mem.reshape(MEM_LEN, d), mem_norm, BF16, tm=MEM_LEN)

    for l in range(depth):
        z, = norm_matmul_rows(xs, norm_mix[l], w_in_pt, l, tm=TM, tn=TN_IN, out_dtype=F32,
                              transposed_b=True, name="in_proj")
        km = matmul_cols(memn, w_mem_k, l, tm=MEM_LEN, tn=TN_MEM, out_dtype=BF16, name="mem_k")
        vm = matmul_cols(memn, w_mem_v, l, tm=MEM_LEN, tn=TN_MEM, out_dtype=BF16, name="mem_v")
        o_gla = gla_branch(z, tri, tri_f, wa_pad, b_gla_a3, gla_norm3, l, t_blk=t_blk)
        o_ml = mlstm_branch(z, tri, tri_f, rep_sel, conv_w, conv_b3, gate_bias, ml_norm3, l, t_blk=t_blk)
        o_mem = memattn_branch(z, km, vm)
        y, w_out_b, w_ff1_b = merge_branches(o_gla, o_ml, o_mem, z, w_branch, w_gate, b_gate, l,
                                             side_cast=(w_out, w_ff1))
        xs = matmul_acc(y, w_out_b, l, xs, tm=TM, tn=TN_OUT, tk=d, name="out_proj")
        hid, w_ff2_b = norm_matmul_rows(xs, norm_ffn[l], w_ff1_b, l, tm=TM, tn=TN_FFN,
                                        out_dtype=BF16, epilogue="relu2", side_cast=(w_ff2,),
                                        name="ffn_up")
        xs = matmul_acc(hid, w_ff2_b, l, xs, tm=TM, tn=TN_FFN, tk=TK_FFN_DOWN, name="ffn_down")
    return rmsnorm(xs, final_norm, F32).reshape(x.shape)
```

```python
import functools

import jax
import jax.numpy as jnp
from jax import lax
from jax.experimental import pallas as pl
from jax.experimental.pallas import tpu as pltpu

F32 = jnp.float32
BF16 = jnp.bfloat16

EPS = 1e-6
CHUNK = 64
N_HEADS = 4
QK_DIM = 128
V_DIM = 256
QK_W = N_HEADS * QK_DIM
V_W = N_HEADS * V_DIM
GLA_RANK = 16
GLA_TAU = 16.0
CONV_W = 4
MEM_LEN = 256
GATE_RANK = 256
N_BRANCH = 3

LANE = 128
SUBLANE = 8
BF16_SUBLANES = 16
VMEM_LIMIT = 56 * 1024 * 1024

TM = 1024
TN_IN, TN_MERGE, TN_OUT, TN_FFN = 768, 1024, 1024, 1024
TK_FFN_DOWN = 4096
T_MIX = 4 * CHUNK
TM_ATTN = 1024
TM_NORM = 512
TN_MEM = 256

COL_GV, COL_GG, COL_MV, COL_MO, COL_XQ = 0, 8, 16, 24, 32
COL_MQK, COL_GQ, COL_GK = 40, 48, 52
COL_GATE, COL_SMALL, IN_COLS = 56, 58, 60
SM_GA, SM_MI, SM_MF = 0, 16, 20


def _cparams(sem):
    return pltpu.CompilerParams(dimension_semantics=sem, vmem_limit_bytes=VMEM_LIMIT)


def _col_block(col, width):
    return col * LANE // width


def _rmsnorm_kernel(x_ref, g_ref, o_ref):
    x = x_ref[...]
    ms = jnp.mean(x * x, axis=-1, keepdims=True)
    o_ref[...] = ((x * lax.rsqrt(ms + EPS)) * g_ref[...]).astype(o_ref.dtype)


def rmsnorm(x, g, out_dtype, tm=TM_NORM):
    m, d = x.shape
    tm = min(tm, m)
    return pl.pallas_call(
        _rmsnorm_kernel,
        grid=(m // tm,),
        in_specs=[pl.BlockSpec((tm, d), lambda i: (i, 0)),
                  pl.BlockSpec((1, d), lambda i: (0, 0))],
        out_specs=pl.BlockSpec((tm, d), lambda i: (i, 0)),
        out_shape=jax.ShapeDtypeStruct((m, d), out_dtype),
        compiler_params=_cparams(("parallel",)),
        name="rmsnorm",
    )(x, g.reshape(1, d))


def _apply_epilogue(acc, epilogue):
    if epilogue == "relu2":
        return jnp.square(jnp.maximum(acc, 0.0))
    return acc


def _side_cast_specs(srcs, layer, n_steps, step_of):
    in_specs, out_specs, out_shapes = [], [], []
    for w in srcs:
        r, c = w.shape[-2:]
        n_blocks = 1 << (n_steps.bit_length() - 1)
        while r % n_blocks or (r // n_blocks) % BF16_SUBLANES:
            n_blocks //= 2
        rows = r // n_blocks

        def block(*g, last=n_blocks - 1):
            return jnp.minimum(step_of(*g), last)

        in_specs.append(pl.BlockSpec((None, rows, c), lambda *g, b=block: (layer, b(*g), 0)))
        out_specs.append(pl.BlockSpec((rows, c), lambda *g, b=block: (b(*g), 0)))
        out_shapes.append(jax.ShapeDtypeStruct((r, c), BF16))
    return in_specs, out_specs, out_shapes


def _side_cast(side_in, side_out):
    for src_ref, dst_ref in zip(side_in, side_out):
        dst_ref[...] = src_ref[...].astype(dst_ref.dtype)


def _weight_spec(b, layer, block, index):
    if b.ndim == 2:
        return pl.BlockSpec(block, index)
    return pl.BlockSpec((None,) + block, lambda *g: (layer,) + index(*g))


def _norm_mm_rows_kernel(*refs, transposed_b, epilogue, n_side, ts, n_slab):
    x_ref, g_ref, b_ref = refs[:3]
    side_in = refs[3:3 + n_side]
    o_ref = refs[3 + n_side]
    side_out = refs[4 + n_side:4 + 2 * n_side]
    h_even, h_odd = refs[4 + 2 * n_side:]
    r = pl.program_id(0)
    slab = jnp.minimum(pl.program_id(1), n_slab - 1)
    dims = (((1,), (1 if transposed_b else 0,)), ((), ()))

    def step(read_ref, write_ref):
        _side_cast(side_in, side_out)
        if read_ref is not None:
            acc = lax.dot_general(read_ref[...], b_ref[...], dims, preferred_element_type=F32)
            o_ref[...] = _apply_epilogue(acc, epilogue).astype(o_ref.dtype)
        x = x_ref[...]
        ms = jnp.mean(x * x, axis=-1, keepdims=True)
        h = ((x * lax.rsqrt(ms + EPS)) * g_ref[...]).astype(write_ref.dtype)
        write_ref[pl.ds(pl.multiple_of(slab * ts, ts), ts), :] = h

    @pl.when(r == 0)
    def _():
        step(None, h_even)

    @pl.when(r % 2 == 1)
    def _():
        step(h_even, h_odd)

    @pl.when(jnp.logical_and(r > 0, r % 2 == 0))
    def _():
        step(h_odd, h_even)


def norm_matmul_rows(x, g, b, layer, *, tm, tn, out_dtype, name, epilogue=None,
                     transposed_b=False, side_cast=()):
    m, kdim = x.shape
    n = b.shape[-2] if transposed_b else b.shape[-1]
    tm = min(tm, m)
    ni, nj = m // tm, n // tn
    n_slab = 1 << (nj.bit_length() - 1)
    ts = tm // n_slab
    def col(r, j):
        return jnp.where(r == 0, 0, j)

    b_spec = (_weight_spec(b, layer, (tn, kdim), lambda r, j: (col(r, j), 0)) if transposed_b
              else _weight_spec(b, layer, (kdim, tn), lambda r, j: (0, col(r, j))))
    s_in, s_out, s_shapes = _side_cast_specs(
        side_cast, layer, ni * nj, lambda r, j: jnp.where(r == 0, 0, (r - 1) * nj + j))
    x_spec = pl.BlockSpec(
        (ts, kdim),
        lambda r, j: (jnp.minimum(r, ni - 1) * n_slab + jnp.minimum(j, n_slab - 1), 0))
    return pl.pallas_call(
        functools.partial(_norm_mm_rows_kernel, transposed_b=transposed_b, epilogue=epilogue,
                          n_side=len(side_cast), ts=ts, n_slab=n_slab),
        grid=(ni + 1, nj),
        in_specs=[x_spec, pl.BlockSpec((1, kdim), lambda r, j: (0, 0)), b_spec] + s_in,
        out_specs=[pl.BlockSpec((tm, tn), lambda r, j: (jnp.maximum(r - 1, 0), col(r, j)))] + s_out,
        out_shape=[jax.ShapeDtypeStruct((m, n), out_dtype)] + s_shapes,
        scratch_shapes=[pltpu.VMEM((tm, kdim), BF16), pltpu.VMEM((tm, kdim), BF16)],
        compiler_params=_cparams(("arbitrary", "arbitrary")),
        name=name,
    )(x, g.reshape(1, kdim), b, *side_cast)


def _mm_cols_kernel(a_ref, b_ref, o_ref, wb_ref, *, epilogue):
    @pl.when(pl.program_id(1) == 0)
    def _():
        wb_ref[...] = b_ref[...].astype(BF16)

    acc = jnp.dot(a_ref[...].astype(BF16), wb_ref[...], preferred_element_type=F32)
    o_ref[...] = _apply_epilogue(acc, epilogue).astype(o_ref.dtype)


def matmul_cols(a, b, layer, *, tm, tn, out_dtype, epilogue=None, name):
    m, kdim = a.shape
    n = b.shape[-1]
    tm = min(tm, m)
    return pl.pallas_call(
        functools.partial(_mm_cols_kernel, epilogue=epilogue),
        grid=(n // tn, m // tm),
        in_specs=[pl.BlockSpec((tm, kdim), lambda j, i: (i, 0)),
                  pl.BlockSpec((None, kdim, tn), lambda j, i: (layer, 0, j))],
        out_specs=pl.BlockSpec((tm, tn), lambda j, i: (i, j)),
        out_shape=jax.ShapeDtypeStruct((m, n), out_dtype),
        scratch_shapes=[pltpu.VMEM((kdim, tn), BF16)],
        compiler_params=_cparams(("parallel", "arbitrary")),
        name=name,
    )(a, b)


def _mm_acc_kernel(a_ref, b_ref, r_ref, o_ref, *, nk):
    def partial_sum():
        return jnp.dot(a_ref[...], b_ref[...], preferred_element_type=F32)

    if nk == 1:
        o_ref[...] = partial_sum() + r_ref[...]
        return
    k = pl.program_id(2)

    @pl.when(k == 0)
    def _():
        o_ref[...] = partial_sum()

    @pl.when(jnp.logical_and(k > 0, k < nk - 1))
    def _():
        o_ref[...] = partial_sum() + o_ref[...]

    @pl.when(k == nk - 1)
    def _():
        o_ref[...] = (partial_sum() + o_ref[...]) + r_ref[...]


def matmul_acc(a, b, layer, residual, *, tm, tn, tk, name):
    m, kdim = a.shape
    n = b.shape[-1]
    tm = min(tm, m)
    return pl.pallas_call(
        functools.partial(_mm_acc_kernel, nk=kdim // tk),
        grid=(m // tm, n // tn, kdim // tk),
        in_specs=[pl.BlockSpec((tm, tk), lambda i, j, k: (i, k)),
                  _weight_spec(b, layer, (tk, tn), lambda i, j, k: (k, j)),
                  pl.BlockSpec((tm, tn), lambda i, j, k: (i, j))],
        out_specs=pl.BlockSpec((tm, tn), lambda i, j, k: (i, j)),
        out_shape=jax.ShapeDtypeStruct((m, n), F32),
        compiler_params=_cparams(("parallel", "parallel", "arbitrary")),
        name=name,
    )(a, b, residual)


def _log_sigmoid(x):
    return jnp.minimum(x, 0.0) - jnp.log1p(jnp.exp(-jnp.abs(x)))


def _sigmoid(x):
    return 1.0 / (1.0 + jnp.exp(-x))


def chunk_tri(t):
    r = lax.broadcasted_iota(jnp.int32, (t, t), 0)
    c = lax.broadcasted_iota(jnp.int32, (t, t), 1)
    same = (r // CHUNK) == (c // CHUNK)
    return jnp.logical_and(same, c <= r).astype(BF16)


def _chunk_cumsum(x, tri):
    hi = x.astype(BF16)
    r1 = x - hi.astype(F32)
    mid = r1.astype(BF16)
    lo = (r1 - mid.astype(F32)).astype(BF16)
    out = jnp.dot(tri, hi, preferred_element_type=F32)
    out += jnp.dot(tri, mid, preferred_element_type=F32)
    out += jnp.dot(tri, lo, preferred_element_type=F32)
    return out


def _split_dot(x, sel):
    hi = x.astype(BF16)
    r1 = x - hi.astype(F32)
    mid = r1.astype(BF16)
    lo = (r1 - mid.astype(F32)).astype(BF16)
    out = jnp.dot(hi, sel, preferred_element_type=F32)
    out += jnp.dot(mid, sel, preferred_element_type=F32)
    out += jnp.dot(lo, sel, preferred_element_type=F32)
    return out


def gate_replicator():
    src = lax.broadcasted_iota(jnp.int32, (LANE, 2 * N_HEADS * LANE), 0)
    grp = lax.broadcasted_iota(jnp.int32, (LANE, 2 * N_HEADS * LANE), 1) // LANE
    want = jnp.where(grp % 2 == 0, SM_MI, SM_MF) + grp // 2
    return (src == want).astype(BF16)


def _dot_nt(a, b):
    return lax.dot_general(a, b, (((1,), (1,)), ((), ())), preferred_element_type=F32)


def _dot_tn(a, b):
    return lax.dot_general(a, b, (((0,), (0,)), ((), ())), preferred_element_type=F32)


def _head_rmsnorm(o, g):
    ms = jnp.mean(o * o, axis=-1, keepdims=True)
    return (o * lax.rsqrt(ms + EPS)) * g


def _gla_kernel(q_ref, k_ref, v_ref, gg_ref, sm_ref, tri_ref, mask_ref, wa_ref, ba_ref, gn_ref,
                o_ref, st_ref, *, t_blk):
    @pl.when(pl.program_id(0) == 0)
    def _():
        st_ref[...] = jnp.zeros_like(st_ref)

    pre = jnp.dot(sm_ref[...].astype(BF16), wa_ref[...],
                  preferred_element_type=F32) + ba_ref[...]
    log_a = _log_sigmoid(pre) / GLA_TAU
    b_all = _chunk_cumsum(log_a, tri_ref[...])
    allowed = mask_ref[...] > 0.5
    scale = QK_DIM ** -0.5
    n_chunks = t_blk // CHUNK
    chunk_rows = [slice(c * CHUNK, (c + 1) * CHUNK) for c in range(n_chunks)]
    heads = range(N_HEADS)
    vcs = [slice(h * V_DIM, (h + 1) * V_DIM) for h in heads]

    q_d, o_intra, d_st, decay = [], [], [], []
    for h in heads:
        kc = slice(h * QK_DIM, (h + 1) * QK_DIM)
        b = b_all[:, kc]
        b_last = [b[r][CHUNK - 1:CHUNK] for r in chunk_rows]
        b_last_rows = jnp.concatenate(
            [jnp.broadcast_to(bl, (CHUNK, QK_DIM)) for bl in b_last], axis=0)
        k = k_ref[:, kc]
        v = v_ref[:, vcs[h]].astype(BF16)
        qd = ((q_ref[:, kc] * scale) * jnp.exp(b)).astype(BF16)
        k_d = (k * jnp.exp(-b)).astype(BF16)
        k_s = (k * jnp.exp(b_last_rows - b)).astype(BF16)
        att = jnp.where(allowed, _dot_nt(qd, k_d), 0.0)
        q_d.append(qd)
        o_intra.append(jnp.dot(att.astype(BF16), v, preferred_element_type=F32))
        d_st.append([_dot_tn(v[r], k_s[r]) for r in chunk_rows])
        decay.append([jnp.exp(bl) for bl in b_last])

    states = [st_ref[h] for h in heads]
    inter = [[] for _ in heads]
    for c, r in enumerate(chunk_rows):
        for h in heads:
            inter[h].append(_dot_nt(q_d[h][r], states[h].astype(BF16)))
            states[h] = states[h] * decay[h][c] + d_st[h][c]
    for h in heads:
        st_ref[h] = states[h]

    for h in heads:
        o = o_intra[h] + jnp.concatenate(inter[h], axis=0)
        gg = gg_ref[:, vcs[h]]
        o = _head_rmsnorm(o, gn_ref[:, vcs[h]]) * (gg * _sigmoid(gg))
        o_ref[:, vcs[h]] = o.astype(o_ref.dtype)


def gla_branch(z, tri, mask, wa_pad, ba, gn, layer, *, t_blk):
    s = z.shape[0]
    return pl.pallas_call(
        functools.partial(_gla_kernel, t_blk=t_blk),
        grid=(s // t_blk,),
        in_specs=[
            pl.BlockSpec((t_blk, QK_W), lambda t: (t, _col_block(COL_GQ, QK_W))),
            pl.BlockSpec((t_blk, QK_W), lambda t: (t, _col_block(COL_GK, QK_W))),
            pl.BlockSpec((t_blk, V_W), lambda t: (t, _col_block(COL_GV, V_W))),
            pl.BlockSpec((t_blk, V_W), lambda t: (t, _col_block(COL_GG, V_W))),
            pl.BlockSpec((t_blk, LANE), lambda t: (t, COL_SMALL)),
            pl.BlockSpec((t_blk, t_blk), lambda t: (0, 0)),
            pl.BlockSpec((t_blk, t_blk), lambda t: (0, 0)),
            pl.BlockSpec((None, LANE, QK_W), lambda t: (layer, 0, 0)),
            pl.BlockSpec((None, 1, QK_W), lambda t: (layer, 0, 0)),
            pl.BlockSpec((None, 1, V_W), lambda t: (layer, 0, 0)),
        ],
        out_specs=pl.BlockSpec((t_blk, V_W), lambda t: (t, 0)),
        out_shape=jax.ShapeDtypeStruct((s, V_W), BF16),
        scratch_shapes=[pltpu.VMEM((N_HEADS, V_DIM, QK_DIM), F32)],
        compiler_params=_cparams(("arbitrary",)),
        name="gla",
    )(z, z, z, z, z, tri, mask, wa_pad, ba, gn)


def _mlstm_kernel(qk_ref, v_ref, mo_ref, sm_ref, tri_ref, mask_ref, rep_ref, cw_ref, cb_ref, gb_ref, mn_ref,
                  o_ref, u_ref, c_ref, n_ref, m_ref, *, t_blk):
    @pl.when(pl.program_id(0) == 0)
    def _():
        u_ref[0:SUBLANE, :] = jnp.zeros((SUBLANE, 2 * QK_W), F32)
        c_ref[...] = jnp.zeros_like(c_ref)
        n_ref[...] = jnp.zeros_like(n_ref)
        m_ref[...] = jnp.full(m_ref.shape, -1e30, F32)

    u_ref[SUBLANE:SUBLANE + t_blk, :] = qk_ref[...]
    y = cb_ref[...]
    for j in range(CONV_W):
        off = SUBLANE - (CONV_W - 1) + j
        y = y + cw_ref[j:j + 1, :] * u_ref[off:off + t_blk, :]
    u_ref[0:SUBLANE, :] = u_ref[t_blk:t_blk + SUBLANE, :]
    qk_all = y * _sigmoid(y)

    pre = sm_ref[...] + gb_ref[...]
    f_cum = _chunk_cumsum(_log_sigmoid(pre), tri_ref[...])
    lane = lax.broadcasted_iota(jnp.int32, (t_blk, LANE), 1)
    is_f = jnp.logical_and(lane >= SM_MF, lane < SM_MF + N_HEADS)
    cols = jnp.where(is_f, f_cum, pre)
    rows_t = cols.T

    allowed = mask_ref[...] > 0.5
    n_chunks = t_blk // CHUNK
    chunk_rows = [slice(c * CHUNK, (c + 1) * CHUNK) for c in range(n_chunks)]
    row_chunk = lax.broadcasted_iota(jnp.int32, (t_blk, LANE), 0) // CHUNK
    lane_chunk = lax.broadcasted_iota(jnp.int32, (1, t_blk), 1) // CHUNK
    rep = _split_dot(cols, rep_ref[...])

    def per_chunk_rep(vals):
        out = jnp.broadcast_to(vals[-1], (t_blk, LANE))
        for c in range(n_chunks - 2, -1, -1):
            out = jnp.where(row_chunk == c, vals[c], out)
        return out

    def lanes2(x):
        return jnp.concatenate([x] * (V_DIM // LANE), axis=1)

    k_scale = QK_DIM ** -0.5
    heads = range(N_HEADS)
    vcs = [slice(h * V_DIM, (h + 1) * V_DIM) for h in heads]
    q = [qk_all[:, h * QK_DIM:(h + 1) * QK_DIM] for h in heads]
    k = [qk_all[:, QK_W + h * QK_DIM:QK_W + (h + 1) * QK_DIM] * k_scale for h in heads]
    qb = [x.astype(BF16) for x in q]
    v = [v_ref[:, vc].astype(BF16) for vc in vcs]
    li_rep = [rep[:, (2 * h) * LANE:(2 * h + 1) * LANE] for h in heads]
    f_rep = [rep[:, (2 * h + 1) * LANE:(2 * h + 2) * LANE] for h in heads]
    lir = [rows_t[SM_MI + h:SM_MI + h + 1, :] for h in heads]
    fr = [rows_t[SM_MF + h:SM_MF + h + 1, :] for h in heads]

    f_last, g, d_c, d_n, lmat, l_max, qk = [], [], [], [], [], [], []
    for h in heads:
        fl = [fr[h][:, (c + 1) * CHUNK - 1:(c + 1) * CHUNK] for c in range(n_chunks)]
        fl_row = fl[-1]
        for c in range(n_chunks - 2, -1, -1):
            fl_row = jnp.where(lane_chunk == c, fl[c], fl_row)
        g_row = fl_row - fr[h] + lir[h]
        gh = [jnp.max(jnp.where(lane_chunk == c, g_row, -jnp.inf), axis=-1, keepdims=True)
              for c in range(n_chunks)]
        fl_g = per_chunk_rep([a - b for a, b in zip(fl, gh)])
        kw = k[h] * jnp.exp(fl_g - f_rep[h] + li_rep[h])
        kwb = kw.astype(BF16)
        f_last.append(fl)
        g.append(gh)
        d_c.append([_dot_tn(kwb[r], v[h][r]) for r in chunk_rows])
        d_n.append([jnp.sum(kw[r], axis=0, keepdims=True) for r in chunk_rows])
        lm = jnp.where(allowed, lanes2(f_rep[h]) + (lir[h] - fr[h]), -jnp.inf)
        lmat.append(lm)
        l_max.append(jnp.max(lm, axis=-1, keepdims=True))
        qk.append(_dot_nt(qb[h], k[h].astype(BF16)))

    m_prev = [m_ref[h][:, 0:1] for h in heads]
    c_prev = [c_ref[h] for h in heads]
    n_prev = [n_ref[h] for h in heads]
    m_in = [[] for _ in heads]
    inter = [[] for _ in heads]
    qn = [[] for _ in heads]
    for c, r in enumerate(chunk_rows):
        for h in heads:
            m_in[h].append(m_prev[h])
            inter[h].append(jnp.dot(qb[h][r], c_prev[h].astype(BF16),
                                    preferred_element_type=F32))
            qn[h].append(jnp.sum(q[h][r] * n_prev[h], axis=-1, keepdims=True))
            m_new = jnp.maximum(f_last[h][c] + m_prev[h], g[h][c])
            a = jnp.exp(f_last[h][c] + m_prev[h] - m_new)
            bb = jnp.exp(g[h][c] - m_new)
            c_prev[h] = a * c_prev[h] + bb * d_c[h][c]
            n_prev[h] = a * n_prev[h] + bb * d_n[h][c]
            m_prev[h] = m_new
    for h in heads:
        c_ref[h] = c_prev[h]
        n_ref[h] = n_prev[h]
        m_ref[h] = jnp.broadcast_to(m_prev[h], (1, LANE))

    for h in heads:
        m_inter = f_rep[h] + per_chunk_rep(m_in[h])
        m = jnp.maximum(l_max[h], m_inter)
        s_mat = qk[h] * jnp.exp(lmat[h] - lanes2(m))
        w_inter = jnp.exp(m_inter - m)
        num = jnp.dot(s_mat.astype(BF16), v[h], preferred_element_type=F32)
        num += lanes2(w_inter) * jnp.concatenate(inter[h], axis=0)
        den = jnp.sum(s_mat, axis=-1, keepdims=True)
        den = den + w_inter * jnp.concatenate(qn[h], axis=0)
        hid = num / lanes2(jnp.maximum(jnp.abs(den), jnp.exp(-m)))
        o = _sigmoid(mo_ref[:, vcs[h]]) * hid
        o_ref[:, vcs[h]] = _head_rmsnorm(o, mn_ref[:, vcs[h]]).astype(o_ref.dtype)


def mlstm_branch(z, tri, mask, rep_sel, conv_w, conv_b, gate_bias, mn, layer, *, t_blk):
    s = z.shape[0]
    return pl.pallas_call(
        functools.partial(_mlstm_kernel, t_blk=t_blk),
        grid=(s // t_blk,),
        in_specs=[
            pl.BlockSpec((t_blk, 2 * QK_W), lambda t: (t, _col_block(COL_MQK, 2 * QK_W))),
            pl.BlockSpec((t_blk, V_W), lambda t: (t, _col_block(COL_MV, V_W))),
            pl.BlockSpec((t_blk, V_W), lambda t: (t, _col_block(COL_MO, V_W))),
            pl.BlockSpec((t_blk, LANE), lambda t: (t, COL_SMALL)),
            pl.BlockSpec((t_blk, t_blk), lambda t: (0, 0)),
            pl.BlockSpec((t_blk, t_blk), lambda t: (0, 0)),
            pl.BlockSpec((LANE, 2 * N_HEADS * LANE), lambda t: (0, 0)),
            pl.BlockSpec((None, CONV_W, 2 * QK_W), lambda t: (layer, 0, 0)),
            pl.BlockSpec((None, 1, 2 * QK_W), lambda t: (layer, 0, 0)),
            pl.BlockSpec((None, 1, LANE), lambda t: (layer, 0, 0)),
            pl.BlockSpec((None, 1, V_W), lambda t: (layer, 0, 0)),
        ],
        out_specs=pl.BlockSpec((t_blk, V_W), lambda t: (t, 0)),
        out_shape=jax.ShapeDtypeStruct((s, V_W), BF16),
        scratch_shapes=[pltpu.VMEM((SUBLANE + t_blk, 2 * QK_W), F32),
                        pltpu.VMEM((N_HEADS, QK_DIM, V_DIM), F32),
                        pltpu.VMEM((N_HEADS, 1, QK_DIM), F32),
                        pltpu.VMEM((N_HEADS, 1, LANE), F32)],
        compiler_params=_cparams(("arbitrary",)),
        name="mlstm",
    )(z, z, z, z, tri, mask, rep_sel, conv_w, conv_b, gate_bias, mn)


def _memattn_kernel(q_ref, km_ref, vm_ref, o_ref):
    scale = V_DIM ** -0.5
    for h in range(N_HEADS):
        cols = slice(h * V_DIM, (h + 1) * V_DIM)
        q = q_ref[:, cols].astype(BF16)
        s = _dot_nt(q, km_ref[:, cols]) * scale
        e = jnp.exp(s - jnp.max(s, axis=-1, keepdims=True))
        p = e / jnp.sum(e, axis=-1, keepdims=True)
        o = jnp.dot(p.astype(BF16), vm_ref[:, cols], preferred_element_type=F32)
        o_ref[:, cols] = o.astype(o_ref.dtype)


def memattn_branch(z, km, vm, *, tm=TM_ATTN):
    s = z.shape[0]
    tm = min(tm, s)
    return pl.pallas_call(
        _memattn_kernel,
        grid=(s // tm,),
        in_specs=[pl.BlockSpec((tm, V_W), lambda i: (i, _col_block(COL_XQ, V_W))),
                  pl.BlockSpec((MEM_LEN, V_W), lambda i: (0, 0)),
                  pl.BlockSpec((MEM_LEN, V_W), lambda i: (0, 0))],
        out_specs=pl.BlockSpec((tm, V_W), lambda i: (i, 0)),
        out_shape=jax.ShapeDtypeStruct((s, V_W), BF16),
        compiler_params=_cparams(("parallel",)),
        name="memattn",
    )(z, km, vm)


def _merge_kernel(*refs, n_side):
    o1_ref, o2_ref, o3_ref, gl_ref, wb_ref, wg_ref, bg_ref = refs[:7]
    side_in = refs[7:7 + n_side]
    y_ref = refs[7 + n_side]
    side_out = refs[8 + n_side:8 + 2 * n_side]
    _side_cast(side_in, side_out)
    w = o1_ref.shape[1]
    gl = gl_ref[...].astype(BF16)
    y = None
    for j, o_ref in enumerate((o1_ref, o2_ref, o3_ref)):
        gate = _sigmoid(jnp.dot(gl, wg_ref[j * GATE_RANK:(j + 1) * GATE_RANK, :],
                                preferred_element_type=F32) + bg_ref[j:j + 1, :])
        term = gate * jnp.dot(o_ref[...], wb_ref[j * w:(j + 1) * w, :],
                              preferred_element_type=F32)
        y = term if y is None else y + term
    y_ref[...] = y.astype(y_ref.dtype)


def merge_branches(o1, o2, o3, z, wb, wg, bg, layer, *, tm=TM, tn=TN_MERGE, side_cast=()):
    s, w = o1.shape
    d = wb.shape[-1]
    tm = min(tm, s)
    grid = (d // tn, s // tm)
    o_spec = pl.BlockSpec((tm, w), lambda j, i: (i, 0))
    s_in, s_out, s_shapes = _side_cast_specs(side_cast, layer, grid[0] * grid[1],
                                             lambda j, i: j * grid[1] + i)
    return pl.pallas_call(
        functools.partial(_merge_kernel, n_side=len(side_cast)),
        grid=grid,
        in_specs=[o_spec, o_spec, o_spec,
                  pl.BlockSpec((tm, GATE_RANK), lambda j, i: (i, _col_block(COL_GATE, GATE_RANK))),
                  pl.BlockSpec((N_BRANCH * w, tn), lambda j, i: (0, j)),
                  pl.BlockSpec((N_BRANCH * GATE_RANK, tn), lambda j, i: (0, j)),
                  pl.BlockSpec((None, N_BRANCH, tn), lambda j, i: (layer, 0, j))] + s_in,
        out_specs=[pl.BlockSpec((tm, tn), lambda j, i: (i, j))] + s_out,
        out_shape=[jax.ShapeDtypeStruct((s, d), BF16)] + s_shapes,
        compiler_params=_cparams(("parallel", "parallel")),
        name="merge",
    )(o1, o2, o3, z, wb, wg, bg, *side_cast)


_IN_GROUPS = ("gq", "gk", "gv", "gg", "ga", "mqk", "mv", "mo", "mi", "mf", "xq", "gate")
_IN_SIZES = (QK_W, QK_W, V_W, V_W, GLA_RANK, 2 * QK_W, V_W, V_W, N_HEADS, N_HEADS, V_W, GATE_RANK)
_IN_SRC = {name: sum(_IN_SIZES[:i]) for i, name in enumerate(_IN_GROUPS)}
_IN_WIDTH = dict(zip(_IN_GROUPS, _IN_SIZES))
_IN_DST = {"gv": COL_GV, "gg": COL_GG, "mv": COL_MV, "mo": COL_MO, "xq": COL_XQ,
           "mqk": COL_MQK, "gq": COL_GQ, "gk": COL_GK, "gate": COL_GATE}
_IN_SMALL = ("ga", "mi", "mf")
assert (SM_GA, SM_MI, SM_MF) == (0, GLA_RANK, GLA_RANK + N_HEADS)


def _permute_wt_kernel(w_ref, o_ref):
    lanes = w_ref.shape[1]
    for name, col in _IN_DST.items():
        src, width = _IN_SRC[name], _IN_WIDTH[name]
        o_ref[col * LANE:col * LANE + width, :] = w_ref[src:src + width, :].astype(o_ref.dtype)
    spans = []
    for n in _IN_SMALL:
        if spans and spans[-1][0] + spans[-1][1] == _IN_SRC[n]:
            spans[-1] = (spans[-1][0], spans[-1][1] + _IN_WIDTH[n])
        else:
            spans.append((_IN_SRC[n], _IN_WIDTH[n]))
    small = [w_ref[src:src + width, :] for src, width in spans]
    used = sum(width for _, width in spans)
    small.append(jnp.zeros(((IN_COLS - COL_SMALL) * LANE - used, lanes), F32))
    o_ref[COL_SMALL * LANE:, :] = jnp.concatenate(small, axis=0).astype(o_ref.dtype)


def permute_w_in_t(w_in_t, *, tc=256):
    depth, n_src, d = w_in_t.shape
    n_dst = IN_COLS * LANE
    return pl.pallas_call(
        _permute_wt_kernel,
        grid=(depth, d // tc),
        in_specs=[pl.BlockSpec((None, n_src, tc), lambda l, i: (l, 0, i))],
        out_specs=pl.BlockSpec((None, n_dst, tc), lambda l, i: (l, 0, i)),
        out_shape=jax.ShapeDtypeStruct((depth, n_dst, d), BF16),
        compiler_params=_cparams(("parallel", "parallel")),
        name="permute_w_in",
    )(w_in_t)


def kernel(x, mem, norm_mix, w_in, w_gla_a, b_gla_a, gla_norm, conv_w, conv_b, b_ml_i, b_ml_f, ml_norm, mem_norm, w_mem_k, w_mem_v, w_branch, w_gate, b_gate, w_out, norm_ffn, w_ff1, w_ff2, final_norm):
    depth = w_in.shape[0]
    _, s, d = x.shape
    xs = x.reshape(s, d)
    t_blk = min(T_MIX, s)

    w_in_pt = permute_w_in_t(jnp.swapaxes(w_in, 1, 2))
    wa_pad = jnp.zeros((depth, LANE, QK_W), F32)
    wa_pad = wa_pad.at[:, SM_GA:SM_GA + GLA_RANK, :].set(w_gla_a).astype(BF16)
    gate_bias = jnp.zeros((depth, 1, LANE), F32)
    gate_bias = gate_bias.at[:, 0, SM_MI:SM_MI + N_HEADS].set(b_ml_i)
    gate_bias = gate_bias.at[:, 0, SM_MF:SM_MF + N_HEADS].set(b_ml_f)
    tri = chunk_tri(t_blk)
    tri_f = tri.astype(F32)
    rep_sel = gate_replicator()
    b_gla_a3 = b_gla_a.reshape(depth, 1, QK_W)
    gla_norm3 = gla_norm.reshape(depth, 1, V_W)
    conv_b3 = conv_b.reshape(depth, 1, 2 * QK_W)
    ml_norm3 = ml_norm.reshape(depth, 1, V_W)
    w_branch2 = w_branch.reshape(depth, N_BRANCH * V_W, d)
    w_gate2 = w_gate.reshape(depth, N_BRANCH * GATE_RANK, d)

    memn = rmsnorm(mem.reshape(MEM_LEN, d), mem_norm, BF16, tm=MEM_LEN)

    for l in range(depth):
        z, wb_b, wg_b, w_out_b = norm_matmul_rows(
            xs, norm_mix[l], w_in_pt, l, tm=TM, tn=TN_IN, out_dtype=F32, transposed_b=True,
            side_cast=(w_branch2, w_gate2, w_out), name="in_proj")
        km = matmul_cols(memn, w_mem_k, l, tm=MEM_LEN, tn=TN_MEM, out_dtype=BF16, name="mem_k")
        vm = matmul_cols(memn, w_mem_v, l, tm=MEM_LEN, tn=TN_MEM, out_dtype=BF16, name="mem_v")
        o_gla = gla_branch(z, tri, tri_f, wa_pad, b_gla_a3, gla_norm3, l, t_blk=t_blk)
        o_ml = mlstm_branch(z, tri, tri_f, rep_sel, conv_w, conv_b3, gate_bias, ml_norm3, l, t_blk=t_blk)
        o_mem = memattn_branch(z, km, vm)
        y, w_ff1_b = merge_branches(o_gla, o_ml, o_mem, z, wb_b, wg_b, b_gate, l,
                                    side_cast=(w_ff1,))
        xs = matmul_acc(y, w_out_b, l, xs, tm=TM, tn=TN_OUT, tk=d, name="out_proj")
        hid, w_ff2_b = norm_matmul_rows(xs, norm_ffn[l], w_ff1_b, l, tm=TM, tn=TN_FFN,
                                        out_dtype=BF16, epilogue="relu2", side_cast=(w_ff2,),
                                        name="ffn_up")
        xs = matmul_acc(hid, w_ff2_b, l, xs, tm=TM, tn=TN_FFN, tk=TK_FFN_DOWN, name="ffn_down")
    return rmsnorm(xs, final_norm, F32).reshape(x.shape)
```

```python
import functools

import jax
import jax.numpy as jnp
from jax import lax
from jax.experimental import pallas as pl
from jax.experimental.pallas import tpu as pltpu

F32 = jnp.float32
BF16 = jnp.bfloat16

EPS = 1e-6
CHUNK = 64
N_HEADS = 4
QK_DIM = 128
V_DIM = 256
QK_W = N_HEADS * QK_DIM
V_W = N_HEADS * V_DIM
GLA_RANK = 16
GLA_TAU = 16.0
CONV_W = 4
MEM_LEN = 256
GATE_RANK = 256
N_BRANCH = 3

LANE = 128
SUBLANE = 8
BF16_SUBLANES = 16
VMEM_LIMIT = 56 * 1024 * 1024

TM = 1024
TN_IN, TN_MERGE, TN_OUT, TN_FFN = 768, 1024, 1024, 1024
TK_FFN_DOWN = 4096
T_MIX = 4 * CHUNK
TM_ATTN = 1024
TM_NORM = 512
TN_MEM = 256

COL_GV, COL_GG, COL_MV, COL_MO, COL_XQ = 0, 8, 16, 24, 32
COL_MQK, COL_GQ, COL_GK = 40, 48, 52
COL_GATE, COL_SMALL, IN_COLS = 56, 58, 60
SM_GA, SM_MI, SM_MF = 0, 16, 20


def _cparams(sem):
    return pltpu.CompilerParams(dimension_semantics=sem, vmem_limit_bytes=VMEM_LIMIT)


def _col_block(col, width):
    return col * LANE // width


def _rmsnorm_kernel(x_ref, g_ref, o_ref):
    x = x_ref[...]
    ms = jnp.mean(x * x, axis=-1, keepdims=True)
    o_ref[...] = ((x * lax.rsqrt(ms + EPS)) * g_ref[...]).astype(o_ref.dtype)


def rmsnorm(x, g, out_dtype, tm=TM_NORM):
    m, d = x.shape
    tm = min(tm, m)
    return pl.pallas_call(
        _rmsnorm_kernel,
        grid=(m // tm,),
        in_specs=[pl.BlockSpec((tm, d), lambda i: (i, 0)),
                  pl.BlockSpec((1, d), lambda i: (0, 0))],
        out_specs=pl.BlockSpec((tm, d), lambda i: (i, 0)),
        out_shape=jax.ShapeDtypeStruct((m, d), out_dtype),
        compiler_params=_cparams(("parallel",)),
        name="rmsnorm",
    )(x, g.reshape(1, d))


def _apply_epilogue(acc, epilogue):
    if epilogue == "relu2":
        return jnp.square(jnp.maximum(acc, 0.0))
    return acc


def _side_cast_specs(srcs, layer, n_steps, step_of):
    in_specs, out_specs, out_shapes = [], [], []
    for w in srcs:
        r, c = w.shape[-2:]
        n_blocks = 1 << (n_steps.bit_length() - 1)
        while r % n_blocks or (r // n_blocks) % BF16_SUBLANES:
            n_blocks //= 2
        rows = r // n_blocks

        def block(*g, last=n_blocks - 1):
            return jnp.minimum(step_of(*g), last)

        in_specs.append(pl.BlockSpec((None, rows, c), lambda *g, b=block: (layer, b(*g), 0)))
        out_specs.append(pl.BlockSpec((rows, c), lambda *g, b=block: (b(*g), 0)))
        out_shapes.append(jax.ShapeDtypeStruct((r, c), BF16))
    return in_specs, out_specs, out_shapes


def _side_cast(side_in, side_out):
    for src_ref, dst_ref in zip(side_in, side_out):
        dst_ref[...] = src_ref[...].astype(dst_ref.dtype)


def _weight_spec(b, layer, block, index):
    if b.ndim == 2:
        return pl.BlockSpec(block, index)
    return pl.BlockSpec((None,) + block, lambda *g: (layer,) + index(*g))


def _norm_mm_rows_kernel(*refs, transposed_b, epilogue, n_side, ts, n_slab):
    x_ref, g_ref, b_ref = refs[:3]
    side_in = refs[3:3 + n_side]
    o_ref = refs[3 + n_side]
    side_out = refs[4 + n_side:4 + 2 * n_side]
    h_even, h_odd = refs[4 + 2 * n_side:]
    r = pl.program_id(0)
    slab = jnp.minimum(pl.program_id(1), n_slab - 1)
    dims = (((1,), (1 if transposed_b else 0,)), ((), ()))

    def step(read_ref, write_ref):
        _side_cast(side_in, side_out)
        if read_ref is not None:
            acc = lax.dot_general(read_ref[...], b_ref[...], dims, preferred_element_type=F32)
            o_ref[...] = _apply_epilogue(acc, epilogue).astype(o_ref.dtype)
        x = x_ref[...]
        ms = jnp.mean(x * x, axis=-1, keepdims=True)
        h = ((x * lax.rsqrt(ms + EPS)) * g_ref[...]).astype(write_ref.dtype)
        write_ref[pl.ds(pl.multiple_of(slab * ts, ts), ts), :] = h

    @pl.when(r == 0)
    def _():
        step(None, h_even)

    @pl.when(r % 2 == 1)
    def _():
        step(h_even, h_odd)

    @pl.when(jnp.logical_and(r > 0, r % 2 == 0))
    def _():
        step(h_odd, h_even)


def norm_matmul_rows(x, g, b, layer, *, tm, tn, out_dtype, name, epilogue=None,
                     transposed_b=False, side_cast=()):
    m, kdim = x.shape
    n = b.shape[-2] if transposed_b else b.shape[-1]
    tm = min(tm, m)
    ni, nj = m // tm, n // tn
    n_slab = 1 << (nj.bit_length() - 1)
    ts = tm // n_slab
    def col(r, j):
        return jnp.where(r == 0, 0, j)

    b_spec = (_weight_spec(b, layer, (tn, kdim), lambda r, j: (col(r, j), 0)) if transposed_b
              else _weight_spec(b, layer, (kdim, tn), lambda r, j: (0, col(r, j))))
    s_in, s_out, s_shapes = _side_cast_specs(
        side_cast, layer, ni * nj, lambda r, j: jnp.where(r == 0, 0, (r - 1) * nj + j))
    x_spec = pl.BlockSpec(
        (ts, kdim),
        lambda r, j: (jnp.minimum(r, ni - 1) * n_slab + jnp.minimum(j, n_slab - 1), 0))
    return pl.pallas_call(
        functools.partial(_norm_mm_rows_kernel, transposed_b=transposed_b, epilogue=epilogue,
                          n_side=len(side_cast), ts=ts, n_slab=n_slab),
        grid=(ni + 1, nj),
        in_specs=[x_spec, pl.BlockSpec((1, kdim), lambda r, j: (0, 0)), b_spec] + s_in,
        out_specs=[pl.BlockSpec((tm, tn), lambda r, j: (jnp.maximum(r - 1, 0), col(r, j)))] + s_out,
        out_shape=[jax.ShapeDtypeStruct((m, n), out_dtype)] + s_shapes,
        scratch_shapes=[pltpu.VMEM((tm, kdim), BF16), pltpu.VMEM((tm, kdim), BF16)],
        compiler_params=_cparams(("arbitrary", "arbitrary")),
        name=name,
    )(x, g.reshape(1, kdim), b, *side_cast)


def _mm_cols_kernel(a_ref, b_ref, o_ref, wb_ref, *, epilogue):
    @pl.when(pl.program_id(1) == 0)
    def _():
        wb_ref[...] = b_ref[...].astype(BF16)

    acc = jnp.dot(a_ref[...].astype(BF16), wb_ref[...], preferred_element_type=F32)
    o_ref[...] = _apply_epilogue(acc, epilogue).astype(o_ref.dtype)


def matmul_cols(a, b, layer, *, tm, tn, out_dtype, epilogue=None, name):
    m, kdim = a.shape
    n = b.shape[-1]
    tm = min(tm, m)
    return pl.pallas_call(
        functools.partial(_mm_cols_kernel, epilogue=epilogue),
        grid=(n // tn, m // tm),
        in_specs=[pl.BlockSpec((tm, kdim), lambda j, i: (i, 0)),
                  pl.BlockSpec((None, kdim, tn), lambda j, i: (layer, 0, j))],
        out_specs=pl.BlockSpec((tm, tn), lambda j, i: (i, j)),
        out_shape=jax.ShapeDtypeStruct((m, n), out_dtype),
        scratch_shapes=[pltpu.VMEM((kdim, tn), BF16)],
        compiler_params=_cparams(("parallel", "arbitrary")),
        name=name,
    )(a, b)


def _mm_acc_kernel(a_ref, b_ref, r_ref, o_ref, *, nk):
    def partial_sum():
        return jnp.dot(a_ref[...], b_ref[...], preferred_element_type=F32)

    if nk == 1:
        o_ref[...] = partial_sum() + r_ref[...]
        return
    k = pl.program_id(2)

    @pl.when(k == 0)
    def _():
        o_ref[...] = partial_sum()

    @pl.when(jnp.logical_and(k > 0, k < nk - 1))
    def _():
        o_ref[...] = partial_sum() + o_ref[...]

    @pl.when(k == nk - 1)
    def _():
        o_ref[...] = (partial_sum() + o_ref[...]) + r_ref[...]


def matmul_acc(a, b, layer, residual, *, tm, tn, tk, name):
    m, kdim = a.shape
    n = b.shape[-1]
    tm = min(tm, m)
    return pl.pallas_call(
        functools.partial(_mm_acc_kernel, nk=kdim // tk),
        grid=(m // tm, n // tn, kdim // tk),
        in_specs=[pl.BlockSpec((tm, tk), lambda i, j, k: (i, k)),
                  _weight_spec(b, layer, (tk, tn), lambda i, j, k: (k, j)),
                  pl.BlockSpec((tm, tn), lambda i, j, k: (i, j))],
        out_specs=pl.BlockSpec((tm, tn), lambda i, j, k: (i, j)),
        out_shape=jax.ShapeDtypeStruct((m, n), F32),
        compiler_params=_cparams(("parallel", "parallel", "arbitrary")),
        name=name,
    )(a, b, residual)


def _log_sigmoid(x):
    return jnp.minimum(x, 0.0) - jnp.log1p(jnp.exp(-jnp.abs(x)))


def _sigmoid(x):
    return 1.0 / (1.0 + jnp.exp(-x))


def chunk_tri(t):
    r = lax.broadcasted_iota(jnp.int32, (t, t), 0)
    c = lax.broadcasted_iota(jnp.int32, (t, t), 1)
    same = (r // CHUNK) == (c // CHUNK)
    return jnp.logical_and(same, c <= r).astype(BF16)


def _chunk_cumsum(x, tri):
    hi = x.astype(BF16)
    r1 = x - hi.astype(F32)
    mid = r1.astype(BF16)
    lo = (r1 - mid.astype(F32)).astype(BF16)
    out = jnp.dot(tri, hi, preferred_element_type=F32)
    out += jnp.dot(tri, mid, preferred_element_type=F32)
    out += jnp.dot(tri, lo, preferred_element_type=F32)
    return out


def _split_dot(x, sel):
    hi = x.astype(BF16)
    r1 = x - hi.astype(F32)
    mid = r1.astype(BF16)
    lo = (r1 - mid.astype(F32)).astype(BF16)
    out = jnp.dot(hi, sel, preferred_element_type=F32)
    out += jnp.dot(mid, sel, preferred_element_type=F32)
    out += jnp.dot(lo, sel, preferred_element_type=F32)
    return out


def gate_replicator():
    src = lax.broadcasted_iota(jnp.int32, (LANE, 2 * N_HEADS * LANE), 0)
    grp = lax.broadcasted_iota(jnp.int32, (LANE, 2 * N_HEADS * LANE), 1) // LANE
    want = jnp.where(grp % 2 == 0, SM_MI, SM_MF) + grp // 2
    return (src == want).astype(BF16)


def _dot_nt(a, b):
    return lax.dot_general(a, b, (((1,), (1,)), ((), ())), preferred_element_type=F32)


def _dot_tn(a, b):
    return lax.dot_general(a, b, (((0,), (0,)), ((), ())), preferred_element_type=F32)


def _head_rmsnorm(o, g):
    ms = jnp.mean(o * o, axis=-1, keepdims=True)
    return (o * lax.rsqrt(ms + EPS)) * g


def _gla_kernel(q_ref, k_ref, v_ref, gg_ref, sm_ref, tri_ref, mask_ref, wa_ref, ba_ref, gn_ref,
                o_ref, st_ref, *, t_blk):
    @pl.when(pl.program_id(0) == 0)
    def _():
        st_ref[...] = jnp.zeros_like(st_ref)

    pre = jnp.dot(sm_ref[...].astype(BF16), wa_ref[...],
                  preferred_element_type=F32) + ba_ref[...]
    log_a = _log_sigmoid(pre) / GLA_TAU
    b_all = _chunk_cumsum(log_a, tri_ref[...])
    allowed = mask_ref[...] > 0.5
    scale = QK_DIM ** -0.5
    n_chunks = t_blk // CHUNK
    chunk_rows = [slice(c * CHUNK, (c + 1) * CHUNK) for c in range(n_chunks)]
    heads = range(N_HEADS)
    vcs = [slice(h * V_DIM, (h + 1) * V_DIM) for h in heads]

    q_d, o_intra, d_st, decay = [], [], [], []
    for h in heads:
        kc = slice(h * QK_DIM, (h + 1) * QK_DIM)
        b = b_all[:, kc]
        b_last = [b[r][CHUNK - 1:CHUNK] for r in chunk_rows]
        b_last_rows = jnp.concatenate(
            [jnp.broadcast_to(bl, (CHUNK, QK_DIM)) for bl in b_last], axis=0)
        k = k_ref[:, kc]
        v = v_ref[:, vcs[h]].astype(BF16)
        qd = ((q_ref[:, kc] * scale) * jnp.exp(b)).astype(BF16)
        k_d = (k * jnp.exp(-b)).astype(BF16)
        k_s = (k * jnp.exp(b_last_rows - b)).astype(BF16)
        att = jnp.where(allowed, _dot_nt(qd, k_d), 0.0)
        q_d.append(qd)
        o_intra.append(jnp.dot(att.astype(BF16), v, preferred_element_type=F32))
        d_st.append([_dot_tn(v[r], k_s[r]) for r in chunk_rows])
        decay.append([jnp.exp(bl) for bl in b_last])

    states = [st_ref[h] for h in heads]
    inter = [[] for _ in heads]
    for c, r in enumerate(chunk_rows):
        for h in heads:
            inter[h].append(_dot_nt(q_d[h][r], states[h].astype(BF16)))
            states[h] = states[h] * decay[h][c] + d_st[h][c]
    for h in heads:
        st_ref[h] = states[h]

    for h in heads:
        o = o_intra[h] + jnp.concatenate(inter[h], axis=0)
        gg = gg_ref[:, vcs[h]]
        o = _head_rmsnorm(o, gn_ref[:, vcs[h]]) * (gg * _sigmoid(gg))
        o_ref[:, vcs[h]] = o.astype(o_ref.dtype)


def gla_branch(z, tri, mask, wa_pad, ba, gn, layer, *, t_blk):
    s = z.shape[0]
    return pl.pallas_call(
        functools.partial(_gla_kernel, t_blk=t_blk),
        grid=(s // t_blk,),
        in_specs=[
            pl.BlockSpec((t_blk, QK_W), lambda t: (t, _col_block(COL_GQ, QK_W))),
            pl.BlockSpec((t_blk, QK_W), lambda t: (t, _col_block(COL_GK, QK_W))),
            pl.BlockSpec((t_blk, V_W), lambda t: (t, _col_block(COL_GV, V_W))),
            pl.BlockSpec((t_blk, V_W), lambda t: (t, _col_block(COL_GG, V_W))),
            pl.BlockSpec((t_blk, LANE), lambda t: (t, COL_SMALL)),
            pl.BlockSpec((t_blk, t_blk), lambda t: (0, 0)),
            pl.BlockSpec((t_blk, t_blk), lambda t: (0, 0)),
            pl.BlockSpec((None, LANE, QK_W), lambda t: (layer, 0, 0)),
            pl.BlockSpec((None, 1, QK_W), lambda t: (layer, 0, 0)),
            pl.BlockSpec((None, 1, V_W), lambda t: (layer, 0, 0)),
        ],
        out_specs=pl.BlockSpec((t_blk, V_W), lambda t: (t, 0)),
        out_shape=jax.ShapeDtypeStruct((s, V_W), BF16),
        scratch_shapes=[pltpu.VMEM((N_HEADS, V_DIM, QK_DIM), F32)],
        compiler_params=_cparams(("arbitrary",)),
        name="gla",
    )(z, z, z, z, z, tri, mask, wa_pad, ba, gn)


def _mlstm_kernel(qk_ref, v_ref, mo_ref, sm_ref, tri_ref, mask_ref, rep_ref, cw_ref, cb_ref, gb_ref, mn_ref,
                  o_ref, u_ref, c_ref, n_ref, m_ref, *, t_blk):
    @pl.when(pl.program_id(0) == 0)
    def _():
        u_ref[0:SUBLANE, :] = jnp.zeros((SUBLANE, 2 * QK_W), F32)
        c_ref[...] = jnp.zeros_like(c_ref)
        n_ref[...] = jnp.zeros_like(n_ref)
        m_ref[...] = jnp.full(m_ref.shape, -1e30, F32)

    u_ref[SUBLANE:SUBLANE + t_blk, :] = qk_ref[...]
    y = cb_ref[...]
    for j in range(CONV_W):
        off = SUBLANE - (CONV_W - 1) + j
        y = y + cw_ref[j:j + 1, :] * u_ref[off:off + t_blk, :]
    u_ref[0:SUBLANE, :] = u_ref[t_blk:t_blk + SUBLANE, :]
    qk_all = y * _sigmoid(y)

    pre = sm_ref[...] + gb_ref[...]
    f_cum = _chunk_cumsum(_log_sigmoid(pre), tri_ref[...])
    lane = lax.broadcasted_iota(jnp.int32, (t_blk, LANE), 1)
    is_f = jnp.logical_and(lane >= SM_MF, lane < SM_MF + N_HEADS)
    cols = jnp.where(is_f, f_cum, pre)
    rows_t = cols.T

    allowed = mask_ref[...] > 0.5
    n_chunks = t_blk // CHUNK
    chunk_rows = [slice(c * CHUNK, (c + 1) * CHUNK) for c in range(n_chunks)]
    row_chunk = lax.broadcasted_iota(jnp.int32, (t_blk, LANE), 0) // CHUNK
    lane_chunk = lax.broadcasted_iota(jnp.int32, (1, t_blk), 1) // CHUNK
    rep = _split_dot(cols, rep_ref[...])

    def per_chunk_rep(vals):
        out = jnp.broadcast_to(vals[-1], (t_blk, LANE))
        for c in range(n_chunks - 2, -1, -1):
            out = jnp.where(row_chunk == c, vals[c], out)
        return out

    def widen(x, width):
        return jnp.concatenate([x] * (width // LANE), axis=1) if width > LANE else x

    k_scale = QK_DIM ** -0.5
    heads = range(N_HEADS)
    vcs = [slice(h * V_DIM, (h + 1) * V_DIM) for h in heads]
    q = [qk_all[:, h * QK_DIM:(h + 1) * QK_DIM] for h in heads]
    k = [qk_all[:, QK_W + h * QK_DIM:QK_W + (h + 1) * QK_DIM] * k_scale for h in heads]
    qb = [x.astype(BF16) for x in q]
    v = [v_ref[:, vc].astype(BF16) for vc in vcs]
    li_rep = [rep[:, (2 * h) * LANE:(2 * h + 1) * LANE] for h in heads]
    f_rep = [rep[:, (2 * h + 1) * LANE:(2 * h + 2) * LANE] for h in heads]
    lir = [rows_t[SM_MI + h:SM_MI + h + 1, :] for h in heads]
    fr = [rows_t[SM_MF + h:SM_MF + h + 1, :] for h in heads]

    f_last, g, d_c, d_n, lmat, l_max, qk = [], [], [], [], [], [], []
    for h in heads:
        fl = [fr[h][:, (c + 1) * CHUNK - 1:(c + 1) * CHUNK] for c in range(n_chunks)]
        fl_row = fl[-1]
        for c in range(n_chunks - 2, -1, -1):
            fl_row = jnp.where(lane_chunk == c, fl[c], fl_row)
        g_row = fl_row - fr[h] + lir[h]
        gh = [jnp.max(jnp.where(lane_chunk == c, g_row, -jnp.inf), axis=-1, keepdims=True)
              for c in range(n_chunks)]
        fl_g = per_chunk_rep([a - b for a, b in zip(fl, gh)])
        kw = k[h] * jnp.exp(fl_g - f_rep[h] + li_rep[h])
        kwb = kw.astype(BF16)
        f_last.append(fl)
        g.append(gh)
        d_c.append([_dot_tn(kwb[r], v[h][r]) for r in chunk_rows])
        d_n.append([jnp.sum(kw[r], axis=0, keepdims=True) for r in chunk_rows])
        lm = jnp.where(allowed, widen(f_rep[h], t_blk) + (lir[h] - fr[h]), -jnp.inf)
        lmat.append(lm)
        l_max.append(jnp.max(lm, axis=-1, keepdims=True))
        qk.append(_dot_nt(qb[h], k[h].astype(BF16)))

    m_prev = [m_ref[h][:, 0:1] for h in heads]
    c_prev = [c_ref[h] for h in heads]
    n_prev = [n_ref[h] for h in heads]
    m_in = [[] for _ in heads]
    inter = [[] for _ in heads]
    qn = [[] for _ in heads]
    for c, r in enumerate(chunk_rows):
        for h in heads:
            m_in[h].append(m_prev[h])
            inter[h].append(jnp.dot(qb[h][r], c_prev[h].astype(BF16),
                                    preferred_element_type=F32))
            qn[h].append(jnp.sum(q[h][r] * n_prev[h], axis=-1, keepdims=True))
            m_new = jnp.maximum(f_last[h][c] + m_prev[h], g[h][c])
            a = jnp.exp(f_last[h][c] + m_prev[h] - m_new)
            bb = jnp.exp(g[h][c] - m_new)
            c_prev[h] = a * c_prev[h] + bb * d_c[h][c]
            n_prev[h] = a * n_prev[h] + bb * d_n[h][c]
            m_prev[h] = m_new
    for h in heads:
        c_ref[h] = c_prev[h]
        n_ref[h] = n_prev[h]
        m_ref[h] = jnp.broadcast_to(m_prev[h], (1, LANE))

    for h in heads:
        m_inter = f_rep[h] + per_chunk_rep(m_in[h])
        m = jnp.maximum(l_max[h], m_inter)
        s_mat = qk[h] * jnp.exp(lmat[h] - widen(m, t_blk))
        w_inter = jnp.exp(m_inter - m)
        num = jnp.dot(s_mat.astype(BF16), v[h], preferred_element_type=F32)
        num += widen(w_inter, V_DIM) * jnp.concatenate(inter[h], axis=0)
        den = jnp.sum(s_mat, axis=-1, keepdims=True)
        den = den + w_inter * jnp.concatenate(qn[h], axis=0)
        hid = num / widen(jnp.maximum(jnp.abs(den), jnp.exp(-m)), V_DIM)
        o = _sigmoid(mo_ref[:, vcs[h]]) * hid
        o_ref[:, vcs[h]] = _head_rmsnorm(o, mn_ref[:, vcs[h]]).astype(o_ref.dtype)


def mlstm_branch(z, tri, mask, rep_sel, conv_w, conv_b, gate_bias, mn, layer, *, t_blk):
    s = z.shape[0]
    return pl.pallas_call(
        functools.partial(_mlstm_kernel, t_blk=t_blk),
        grid=(s // t_blk,),
        in_specs=[
            pl.BlockSpec((t_blk, 2 * QK_W), lambda t: (t, _col_block(COL_MQK, 2 * QK_W))),
            pl.BlockSpec((t_blk, V_W), lambda t: (t, _col_block(COL_MV, V_W))),
            pl.BlockSpec((t_blk, V_W), lambda t: (t, _col_block(COL_MO, V_W))),
            pl.BlockSpec((t_blk, LANE), lambda t: (t, COL_SMALL)),
            pl.BlockSpec((t_blk, t_blk), lambda t: (0, 0)),
            pl.BlockSpec((t_blk, t_blk), lambda t: (0, 0)),
            pl.BlockSpec((LANE, 2 * N_HEADS * LANE), lambda t: (0, 0)),
            pl.BlockSpec((None, CONV_W, 2 * QK_W), lambda t: (layer, 0, 0)),
            pl.BlockSpec((None, 1, 2 * QK_W), lambda t: (layer, 0, 0)),
            pl.BlockSpec((None, 1, LANE), lambda t: (layer, 0, 0)),
            pl.BlockSpec((None, 1, V_W), lambda t: (layer, 0, 0)),
        ],
        out_specs=pl.BlockSpec((t_blk, V_W), lambda t: (t, 0)),
        out_shape=jax.ShapeDtypeStruct((s, V_W), BF16),
        scratch_shapes=[pltpu.VMEM((SUBLANE + t_blk, 2 * QK_W), F32),
                        pltpu.VMEM((N_HEADS, QK_DIM, V_DIM), F32),
                        pltpu.VMEM((N_HEADS, 1, QK_DIM), F32),
                        pltpu.VMEM((N_HEADS, 1, LANE), F32)],
        compiler_params=_cparams(("arbitrary",)),
        name="mlstm",
    )(z, z, z, z, tri, mask, rep_sel, conv_w, conv_b, gate_bias, mn)


def _memattn_kernel(q_ref, km_ref, vm_ref, o_ref):
    scale = V_DIM ** -0.5
    for h in range(N_HEADS):
        cols = slice(h * V_DIM, (h + 1) * V_DIM)
        q = q_ref[:, cols].astype(BF16)
        s = _dot_nt(q, km_ref[:, cols]) * scale
        e = jnp.exp(s - jnp.max(s, axis=-1, keepdims=True))
        p = e / jnp.sum(e, axis=-1, keepdims=True)
        o = jnp.dot(p.astype(BF16), vm_ref[:, cols], preferred_element_type=F32)
        o_ref[:, cols] = o.astype(o_ref.dtype)


def memattn_branch(z, km, vm, *, tm=TM_ATTN):
    s = z.shape[0]
    tm = min(tm, s)
    return pl.pallas_call(
        _memattn_kernel,
        grid=(s // tm,),
        in_specs=[pl.BlockSpec((tm, V_W), lambda i: (i, _col_block(COL_XQ, V_W))),
                  pl.BlockSpec((MEM_LEN, V_W), lambda i: (0, 0)),
                  pl.BlockSpec((MEM_LEN, V_W), lambda i: (0, 0))],
        out_specs=pl.BlockSpec((tm, V_W), lambda i: (i, 0)),
        out_shape=jax.ShapeDtypeStruct((s, V_W), BF16),
        compiler_params=_cparams(("parallel",)),
        name="memattn",
    )(z, km, vm)


def _merge_kernel(*refs, n_side):
    o1_ref, o2_ref, o3_ref, gl_ref, wb_ref, wg_ref, bg_ref = refs[:7]
    side_in = refs[7:7 + n_side]
    y_ref = refs[7 + n_side]
    side_out = refs[8 + n_side:8 + 2 * n_side]
    _side_cast(side_in, side_out)
    w = o1_ref.shape[1]
    gl = gl_ref[...].astype(BF16)
    y = None
    for j, o_ref in enumerate((o1_ref, o2_ref, o3_ref)):
        gate = _sigmoid(jnp.dot(gl, wg_ref[j * GATE_RANK:(j + 1) * GATE_RANK, :],
                                preferred_element_type=F32) + bg_ref[j:j + 1, :])
        term = gate * jnp.dot(o_ref[...], wb_ref[j * w:(j + 1) * w, :],
                              preferred_element_type=F32)
        y = term if y is None else y + term
    y_ref[...] = y.astype(y_ref.dtype)


def merge_branches(o1, o2, o3, z, wb, wg, bg, layer, *, tm=TM, tn=TN_MERGE, side_cast=()):
    s, w = o1.shape
    d = wb.shape[-1]
    tm = min(tm, s)
    grid = (d // tn, s // tm)
    o_spec = pl.BlockSpec((tm, w), lambda j, i: (i, 0))
    s_in, s_out, s_shapes = _side_cast_specs(side_cast, layer, grid[0] * grid[1],
                                             lambda j, i: j * grid[1] + i)
    return pl.pallas_call(
        functools.partial(_merge_kernel, n_side=len(side_cast)),
        grid=grid,
        in_specs=[o_spec, o_spec, o_spec,
                  pl.BlockSpec((tm, GATE_RANK), lambda j, i: (i, _col_block(COL_GATE, GATE_RANK))),
                  pl.BlockSpec((N_BRANCH * w, tn), lambda j, i: (0, j)),
                  pl.BlockSpec((N_BRANCH * GATE_RANK, tn), lambda j, i: (0, j)),
                  pl.BlockSpec((None, N_BRANCH, tn), lambda j, i: (layer, 0, j))] + s_in,
        out_specs=[pl.BlockSpec((tm, tn), lambda j, i: (i, j))] + s_out,
        out_shape=[jax.ShapeDtypeStruct((s, d), BF16)] + s_shapes,
        compiler_params=_cparams(("parallel", "parallel")),
        name="merge",
    )(o1, o2, o3, z, wb, wg, bg, *side_cast)


_IN_GROUPS = ("gq", "gk", "gv", "gg", "ga", "mqk", "mv", "mo", "mi", "mf", "xq", "gate")
_IN_SIZES = (QK_W, QK_W, V_W, V_W, GLA_RANK, 2 * QK_W, V_W, V_W, N_HEADS, N_HEADS, V_W, GATE_RANK)
_IN_SRC = {name: sum(_IN_SIZES[:i]) for i, name in enumerate(_IN_GROUPS)}
_IN_WIDTH = dict(zip(_IN_GROUPS, _IN_SIZES))
_IN_DST = {"gv": COL_GV, "gg": COL_GG, "mv": COL_MV, "mo": COL_MO, "xq": COL_XQ,
           "mqk": COL_MQK, "gq": COL_GQ, "gk": COL_GK, "gate": COL_GATE}
_IN_SMALL = ("ga", "mi", "mf")
assert (SM_GA, SM_MI, SM_MF) == (0, GLA_RANK, GLA_RANK + N_HEADS)


def _permute_wt_kernel(w_ref, o_ref):
    lanes = w_ref.shape[1]
    for name, col in _IN_DST.items():
        src, width = _IN_SRC[name], _IN_WIDTH[name]
        o_ref[col * LANE:col * LANE + width, :] = w_ref[src:src + width, :].astype(o_ref.dtype)
    spans = []
    for n in _IN_SMALL:
        if spans and spans[-1][0] + spans[-1][1] == _IN_SRC[n]:
            spans[-1] = (spans[-1][0], spans[-1][1] + _IN_WIDTH[n])
        else:
            spans.append((_IN_SRC[n], _IN_WIDTH[n]))
    small = [w_ref[src:src + width, :] for src, width in spans]
    used = sum(width for _, width in spans)
    small.append(jnp.zeros(((IN_COLS - COL_SMALL) * LANE - used, lanes), F32))
    o_ref[COL_SMALL * LANE:, :] = jnp.concatenate(small, axis=0).astype(o_ref.dtype)


def permute_w_in_t(w_in_t, *, tc=256):
    depth, n_src, d = w_in_t.shape
    n_dst = IN_COLS * LANE
    return pl.pallas_call(
        _permute_wt_kernel,
        grid=(depth, d // tc),
        in_specs=[pl.BlockSpec((None, n_src, tc), lambda l, i: (l, 0, i))],
        out_specs=pl.BlockSpec((None, n_dst, tc), lambda l, i: (l, 0, i)),
        out_shape=jax.ShapeDtypeStruct((depth, n_dst, d), BF16),
        compiler_params=_cparams(("parallel", "parallel")),
        name="permute_w_in",
    )(w_in_t)


def kernel(x, mem, norm_mix, w_in, w_gla_a, b_gla_a, gla_norm, conv_w, conv_b, b_ml_i, b_ml_f, ml_norm, mem_norm, w_mem_k, w_mem_v, w_branch, w_gate, b_gate, w_out, norm_ffn, w_ff1, w_ff2, final_norm):
    depth = w_in.shape[0]
    _, s, d = x.shape
    xs = x.reshape(s, d)
    t_blk = min(T_MIX, s)

    w_in_pt = permute_w_in_t(jnp.swapaxes(w_in, 1, 2))
    wa_pad = jnp.zeros((depth, LANE, QK_W), F32)
    wa_pad = wa_pad.at[:, SM_GA:SM_GA + GLA_RANK, :].set(w_gla_a).astype(BF16)
    gate_bias = jnp.zeros((depth, 1, LANE), F32)
    gate_bias = gate_bias.at[:, 0, SM_MI:SM_MI + N_HEADS].set(b_ml_i)
    gate_bias = gate_bias.at[:, 0, SM_MF:SM_MF + N_HEADS].set(b_ml_f)
    tri = chunk_tri(t_blk)
    tri_f = tri.astype(F32)
    rep_sel = gate_replicator()
    b_gla_a3 = b_gla_a.reshape(depth, 1, QK_W)
    gla_norm3 = gla_norm.reshape(depth, 1, V_W)
    conv_b3 = conv_b.reshape(depth, 1, 2 * QK_W)
    ml_norm3 = ml_norm.reshape(depth, 1, V_W)
    w_branch2 = w_branch.reshape(depth, N_BRANCH * V_W, d)
    w_gate2 = w_gate.reshape(depth, N_BRANCH * GATE_RANK, d)

    memn = rmsnorm(mem.reshape(MEM_LEN, d), mem_norm, BF16, tm=MEM_LEN)

    for l in range(depth):
        z, wb_b, wg_b, w_out_b = norm_matmul_rows(
            xs, norm_mix[l], w_in_pt, l, tm=TM, tn=TN_IN, out_dtype=F32, transposed_b=True,
            side_cast=(w_branch2, w_gate2, w_out), name="in_proj")
        km = matmul_cols(memn, w_mem_k, l, tm=MEM_LEN, tn=TN_MEM, out_dtype=BF16, name="mem_k")
        vm = matmul_cols(memn, w_mem_v, l, tm=MEM_LEN, tn=TN_MEM, out_dtype=BF16, name="mem_v")
        o_gla = gla_branch(z, tri, tri_f, wa_pad, b_gla_a3, gla_norm3, l, t_blk=t_blk)
        o_ml = mlstm_branch(z, tri, tri_f, rep_sel, conv_w, conv_b3, gate_bias, ml_norm3, l, t_blk=t_blk)
        o_mem = memattn_branch(z, km, vm)
        y, w_ff1_b = merge_branches(o_gla, o_ml, o_mem, z, wb_b, wg_b, b_gate, l,
                                    side_cast=(w_ff1,))
        xs = matmul_acc(y, w_out_b, l, xs, tm=TM, tn=TN_OUT, tk=d, name="out_proj")
        hid, w_ff2_b = norm_matmul_rows(xs, norm_ffn[l], w_ff1_b, l, tm=TM, tn=TN_FFN,
                                        out_dtype=BF16, epilogue="relu2", side_cast=(w_ff2,),
                                        name="ffn_up")
        xs = matmul_acc(hid, w_ff2_b, l, xs, tm=TM, tn=TN_FFN, tk=TK_FFN_DOWN, name="ffn_down")
    return rmsnorm(xs, final_norm, F32).reshape(x.shape)
```

```python
import functools

import jax
import jax.numpy as jnp
from jax import lax
from jax.experimental import pallas as pl
from jax.experimental.pallas import tpu as pltpu

F32 = jnp.float32
BF16 = jnp.bfloat16

EPS = 1e-6
CHUNK = 64
N_HEADS = 4
QK_DIM = 128
V_DIM = 256
QK_W = N_HEADS * QK_DIM
V_W = N_HEADS * V_DIM
GLA_RANK = 16
GLA_TAU = 16.0
CONV_W = 4
MEM_LEN = 256
GATE_RANK = 256
N_BRANCH = 3

LANE = 128
SUBLANE = 8
BF16_SUBLANES = 16
VMEM_LIMIT = 56 * 1024 * 1024

TM = 1024
TN_IN, TN_MERGE, TN_OUT, TN_FFN = 768, 1024, 1024, 1024
TK_FFN_DOWN = 4096
T_MIX = 4 * CHUNK
TM_ATTN = 1024
TM_NORM = 512
TN_MEM = 256
MERGE_STRIP = 512

COL_GV, COL_GG, COL_MV, COL_MO, COL_XQ = 0, 8, 16, 24, 32
COL_MQK, COL_GQ, COL_GK = 40, 48, 52
COL_GATE, COL_SMALL, IN_COLS = 56, 58, 60
SM_GA, SM_MI, SM_MF = 0, 16, 20


def _cparams(sem):
    return pltpu.CompilerParams(dimension_semantics=sem, vmem_limit_bytes=VMEM_LIMIT)


def _col_block(col, width):
    return col * LANE // width


def _rmsnorm_kernel(x_ref, g_ref, o_ref):
    x = x_ref[...]
    ms = jnp.mean(x * x, axis=-1, keepdims=True)
    o_ref[...] = ((x * lax.rsqrt(ms + EPS)) * g_ref[...]).astype(o_ref.dtype)


def rmsnorm(x, g, out_dtype, tm=TM_NORM):
    m, d = x.shape
    tm = min(tm, m)
    return pl.pallas_call(
        _rmsnorm_kernel,
        grid=(m // tm,),
        in_specs=[pl.BlockSpec((tm, d), lambda i: (i, 0)),
                  pl.BlockSpec((1, d), lambda i: (0, 0))],
        out_specs=pl.BlockSpec((tm, d), lambda i: (i, 0)),
        out_shape=jax.ShapeDtypeStruct((m, d), out_dtype),
        compiler_params=_cparams(("parallel",)),
        name="rmsnorm",
    )(x, g.reshape(1, d))


def _apply_epilogue(acc, epilogue):
    if epilogue == "relu2":
        return jnp.square(jnp.maximum(acc, 0.0))
    return acc


def _side_cast_specs(srcs, layer, n_steps, step_of):
    in_specs, out_specs, out_shapes = [], [], []
    for w in srcs:
        r, c = w.shape[-2:]
        n_blocks = 1 << (n_steps.bit_length() - 1)
        while r % n_blocks or (r // n_blocks) % BF16_SUBLANES:
            n_blocks //= 2
        rows = r // n_blocks

        def block(*g, last=n_blocks - 1):
            return jnp.minimum(step_of(*g), last)

        in_specs.append(pl.BlockSpec((None, rows, c), lambda *g, b=block: (layer, b(*g), 0)))
        out_specs.append(pl.BlockSpec((rows, c), lambda *g, b=block: (b(*g), 0)))
        out_shapes.append(jax.ShapeDtypeStruct((r, c), BF16))
    return in_specs, out_specs, out_shapes


def _side_cast(side_in, side_out):
    for src_ref, dst_ref in zip(side_in, side_out):
        dst_ref[...] = src_ref[...].astype(dst_ref.dtype)


def _weight_spec(b, layer, block, index):
    if b.ndim == 2:
        return pl.BlockSpec(block, index)
    return pl.BlockSpec((None,) + block, lambda *g: (layer,) + index(*g))


def _norm_mm_rows_kernel(*refs, transposed_b, epilogue, n_side, ts, n_slab):
    x_ref, g_ref, b_ref = refs[:3]
    side_in = refs[3:3 + n_side]
    o_ref = refs[3 + n_side]
    side_out = refs[4 + n_side:4 + 2 * n_side]
    h_even, h_odd = refs[4 + 2 * n_side:]
    r = pl.program_id(0)
    slab = jnp.minimum(pl.program_id(1), n_slab - 1)
    dims = (((1,), (1 if transposed_b else 0,)), ((), ()))

    def step(read_ref, write_ref):
        _side_cast(side_in, side_out)
        if read_ref is not None:
            acc = lax.dot_general(read_ref[...], b_ref[...], dims, preferred_element_type=F32)
            o_ref[...] = _apply_epilogue(acc, epilogue).astype(o_ref.dtype)
        x = x_ref[...]
        ms = jnp.mean(x * x, axis=-1, keepdims=True)
        h = ((x * lax.rsqrt(ms + EPS)) * g_ref[...]).astype(write_ref.dtype)
        write_ref[pl.ds(pl.multiple_of(slab * ts, ts), ts), :] = h

    @pl.when(r == 0)
    def _():
        step(None, h_even)

    @pl.when(r % 2 == 1)
    def _():
        step(h_even, h_odd)

    @pl.when(jnp.logical_and(r > 0, r % 2 == 0))
    def _():
        step(h_odd, h_even)


def norm_matmul_rows(x, g, b, layer, *, tm, tn, out_dtype, name, epilogue=None,
                     transposed_b=False, side_cast=()):
    m, kdim = x.shape
    n = b.shape[-2] if transposed_b else b.shape[-1]
    tm = min(tm, m)
    ni, nj = m // tm, n // tn
    n_slab = 1 << (nj.bit_length() - 1)
    ts = tm // n_slab
    def col(r, j):
        return jnp.where(r == 0, 0, j)

    b_spec = (_weight_spec(b, layer, (tn, kdim), lambda r, j: (col(r, j), 0)) if transposed_b
              else _weight_spec(b, layer, (kdim, tn), lambda r, j: (0, col(r, j))))
    s_in, s_out, s_shapes = _side_cast_specs(
        side_cast, layer, ni * nj, lambda r, j: jnp.where(r == 0, 0, (r - 1) * nj + j))
    x_spec = pl.BlockSpec(
        (ts, kdim),
        lambda r, j: (jnp.minimum(r, ni - 1) * n_slab + jnp.minimum(j, n_slab - 1), 0))
    return pl.pallas_call(
        functools.partial(_norm_mm_rows_kernel, transposed_b=transposed_b, epilogue=epilogue,
                          n_side=len(side_cast), ts=ts, n_slab=n_slab),
        grid=(ni + 1, nj),
        in_specs=[x_spec, pl.BlockSpec((1, kdim), lambda r, j: (0, 0)), b_spec] + s_in,
        out_specs=[pl.BlockSpec((tm, tn), lambda r, j: (jnp.maximum(r - 1, 0), col(r, j)))] + s_out,
        out_shape=[jax.ShapeDtypeStruct((m, n), out_dtype)] + s_shapes,
        scratch_shapes=[pltpu.VMEM((tm, kdim), BF16), pltpu.VMEM((tm, kdim), BF16)],
        compiler_params=_cparams(("arbitrary", "arbitrary")),
        name=name,
    )(x, g.reshape(1, kdim), b, *side_cast)


def _mm_cols_kernel(a_ref, b_ref, o_ref, wb_ref, *, epilogue):
    @pl.when(pl.program_id(1) == 0)
    def _():
        wb_ref[...] = b_ref[...].astype(BF16)

    acc = jnp.dot(a_ref[...].astype(BF16), wb_ref[...], preferred_element_type=F32)
    o_ref[...] = _apply_epilogue(acc, epilogue).astype(o_ref.dtype)


def matmul_cols(a, b, layer, *, tm, tn, out_dtype, epilogue=None, name):
    m, kdim = a.shape
    n = b.shape[-1]
    tm = min(tm, m)
    return pl.pallas_call(
        functools.partial(_mm_cols_kernel, epilogue=epilogue),
        grid=(n // tn, m // tm),
        in_specs=[pl.BlockSpec((tm, kdim), lambda j, i: (i, 0)),
                  pl.BlockSpec((None, kdim, tn), lambda j, i: (layer, 0, j))],
        out_specs=pl.BlockSpec((tm, tn), lambda j, i: (i, j)),
        out_shape=jax.ShapeDtypeStruct((m, n), out_dtype),
        scratch_shapes=[pltpu.VMEM((kdim, tn), BF16)],
        compiler_params=_cparams(("parallel", "arbitrary")),
        name=name,
    )(a, b)


def _mm_acc_kernel(a_ref, b_ref, r_ref, o_ref, *, nk):
    def partial_sum():
        return jnp.dot(a_ref[...], b_ref[...], preferred_element_type=F32)

    if nk == 1:
        o_ref[...] = partial_sum() + r_ref[...]
        return
    k = pl.program_id(2)

    @pl.when(k == 0)
    def _():
        o_ref[...] = partial_sum()

    @pl.when(jnp.logical_and(k > 0, k < nk - 1))
    def _():
        o_ref[...] = partial_sum() + o_ref[...]

    @pl.when(k == nk - 1)
    def _():
        o_ref[...] = (partial_sum() + o_ref[...]) + r_ref[...]


def matmul_acc(a, b, layer, residual, *, tm, tn, tk, name):
    m, kdim = a.shape
    n = b.shape[-1]
    tm = min(tm, m)
    return pl.pallas_call(
        functools.partial(_mm_acc_kernel, nk=kdim // tk),
        grid=(m // tm, n // tn, kdim // tk),
        in_specs=[pl.BlockSpec((tm, tk), lambda i, j, k: (i, k)),
                  _weight_spec(b, layer, (tk, tn), lambda i, j, k: (k, j)),
                  pl.BlockSpec((tm, tn), lambda i, j, k: (i, j))],
        out_specs=pl.BlockSpec((tm, tn), lambda i, j, k: (i, j)),
        out_shape=jax.ShapeDtypeStruct((m, n), F32),
        compiler_params=_cparams(("parallel", "parallel", "arbitrary")),
        name=name,
    )(a, b, residual)


def _log_sigmoid(x):
    return jnp.minimum(x, 0.0) - jnp.log1p(jnp.exp(-jnp.abs(x)))


def _sigmoid(x):
    return 1.0 / (1.0 + jnp.exp(-x))


def chunk_tri(t):
    r = lax.broadcasted_iota(jnp.int32, (t, t), 0)
    c = lax.broadcasted_iota(jnp.int32, (t, t), 1)
    same = (r // CHUNK) == (c // CHUNK)
    return jnp.logical_and(same, c <= r).astype(BF16)


def _chunk_cumsum(x, tri):
    hi = x.astype(BF16)
    r1 = x - hi.astype(F32)
    mid = r1.astype(BF16)
    lo = (r1 - mid.astype(F32)).astype(BF16)
    out = jnp.dot(tri, hi, preferred_element_type=F32)
    out += jnp.dot(tri, mid, preferred_element_type=F32)
    out += jnp.dot(tri, lo, preferred_element_type=F32)
    return out


def _split_dot(x, sel):
    hi = x.astype(BF16)
    r1 = x - hi.astype(F32)
    mid = r1.astype(BF16)
    lo = (r1 - mid.astype(F32)).astype(BF16)
    out = jnp.dot(hi, sel, preferred_element_type=F32)
    out += jnp.dot(mid, sel, preferred_element_type=F32)
    out += jnp.dot(lo, sel, preferred_element_type=F32)
    return out


def gate_replicator():
    src = lax.broadcasted_iota(jnp.int32, (LANE, 2 * N_HEADS * LANE), 0)
    grp = lax.broadcasted_iota(jnp.int32, (LANE, 2 * N_HEADS * LANE), 1) // LANE
    want = jnp.where(grp % 2 == 0, SM_MI, SM_MF) + grp // 2
    return (src == want).astype(BF16)


def _dot_nt(a, b):
    return lax.dot_general(a, b, (((1,), (1,)), ((), ())), preferred_element_type=F32)


def _dot_tn(a, b):
    return lax.dot_general(a, b, (((0,), (0,)), ((), ())), preferred_element_type=F32)


def _head_rmsnorm(o, g):
    ms = jnp.mean(o * o, axis=-1, keepdims=True)
    return (o * lax.rsqrt(ms + EPS)) * g


def _gla_kernel(q_ref, k_ref, v_ref, gg_ref, sm_ref, tri_ref, mask_ref, wa_ref, ba_ref, gn_ref,
                o_ref, st_ref, *, t_blk):
    @pl.when(pl.program_id(0) == 0)
    def _():
        st_ref[...] = jnp.zeros_like(st_ref)

    pre = jnp.dot(sm_ref[...].astype(BF16), wa_ref[...],
                  preferred_element_type=F32) + ba_ref[...]
    log_a = _log_sigmoid(pre) / GLA_TAU
    b_all = _chunk_cumsum(log_a, tri_ref[...])
    allowed = mask_ref[...] > 0.5
    scale = QK_DIM ** -0.5
    n_chunks = t_blk // CHUNK
    chunk_rows = [slice(c * CHUNK, (c + 1) * CHUNK) for c in range(n_chunks)]
    heads = range(N_HEADS)
    vcs = [slice(h * V_DIM, (h + 1) * V_DIM) for h in heads]

    q_d, o_intra, d_st, decay = [], [], [], []
    for h in heads:
        kc = slice(h * QK_DIM, (h + 1) * QK_DIM)
        b = b_all[:, kc]
        b_last = [b[r][CHUNK - 1:CHUNK] for r in chunk_rows]
        b_last_rows = jnp.concatenate(
            [jnp.broadcast_to(bl, (CHUNK, QK_DIM)) for bl in b_last], axis=0)
        k = k_ref[:, kc]
        v = v_ref[:, vcs[h]].astype(BF16)
        qd = ((q_ref[:, kc] * scale) * jnp.exp(b)).astype(BF16)
        k_d = (k * jnp.exp(-b)).astype(BF16)
        k_s = (k * jnp.exp(b_last_rows - b)).astype(BF16)
        att = jnp.where(allowed, _dot_nt(qd, k_d), 0.0)
        q_d.append(qd)
        o_intra.append(jnp.dot(att.astype(BF16), v, preferred_element_type=F32))
        d_st.append([_dot_tn(v[r], k_s[r]) for r in chunk_rows])
        decay.append([jnp.exp(bl) for bl in b_last])

    states = [st_ref[h] for h in heads]
    inter = [[] for _ in heads]
    for c, r in enumerate(chunk_rows):
        for h in heads:
            inter[h].append(_dot_nt(q_d[h][r], states[h].astype(BF16)))
            states[h] = states[h] * decay[h][c] + d_st[h][c]
    for h in heads:
        st_ref[h] = states[h]

    for h in heads:
        o = o_intra[h] + jnp.concatenate(inter[h], axis=0)
        gg = gg_ref[:, vcs[h]]
        o = _head_rmsnorm(o, gn_ref[:, vcs[h]]) * (gg * _sigmoid(gg))
        o_ref[:, vcs[h]] = o.astype(o_ref.dtype)


def gla_branch(z, tri, mask, wa_pad, ba, gn, layer, *, t_blk):
    s = z.shape[0]
    return pl.pallas_call(
        functools.partial(_gla_kernel, t_blk=t_blk),
        grid=(s // t_blk,),
        in_specs=[
            pl.BlockSpec((t_blk, QK_W), lambda t: (t, _col_block(COL_GQ, QK_W))),
            pl.BlockSpec((t_blk, QK_W), lambda t: (t, _col_block(COL_GK, QK_W))),
            pl.BlockSpec((t_blk, V_W), lambda t: (t, _col_block(COL_GV, V_W))),
            pl.BlockSpec((t_blk, V_W), lambda t: (t, _col_block(COL_GG, V_W))),
            pl.BlockSpec((t_blk, LANE), lambda t: (t, COL_SMALL)),
            pl.BlockSpec((t_blk, t_blk), lambda t: (0, 0)),
            pl.BlockSpec((t_blk, t_blk), lambda t: (0, 0)),
            pl.BlockSpec((None, LANE, QK_W), lambda t: (layer, 0, 0)),
            pl.BlockSpec((None, 1, QK_W), lambda t: (layer, 0, 0)),
            pl.BlockSpec((None, 1, V_W), lambda t: (layer, 0, 0)),
        ],
        out_specs=pl.BlockSpec((t_blk, V_W), lambda t: (t, 0)),
        out_shape=jax.ShapeDtypeStruct((s, V_W), BF16),
        scratch_shapes=[pltpu.VMEM((N_HEADS, V_DIM, QK_DIM), F32)],
        compiler_params=_cparams(("arbitrary",)),
        name="gla",
    )(z, z, z, z, z, tri, mask, wa_pad, ba, gn)


def _mlstm_kernel(qk_ref, v_ref, mo_ref, sm_ref, tri_ref, mask_ref, rep_ref, cw_ref, cb_ref, gb_ref, mn_ref,
                  o_ref, u_ref, c_ref, n_ref, m_ref, *, t_blk):
    @pl.when(pl.program_id(0) == 0)
    def _():
        u_ref[0:SUBLANE, :] = jnp.zeros((SUBLANE, 2 * QK_W), F32)
        c_ref[...] = jnp.zeros_like(c_ref)
        n_ref[...] = jnp.zeros_like(n_ref)
        m_ref[...] = jnp.full(m_ref.shape, -1e30, F32)

    u_ref[SUBLANE:SUBLANE + t_blk, :] = qk_ref[...]
    y = cb_ref[...]
    for j in range(CONV_W):
        off = SUBLANE - (CONV_W - 1) + j
        y = y + cw_ref[j:j + 1, :] * u_ref[off:off + t_blk, :]
    u_ref[0:SUBLANE, :] = u_ref[t_blk:t_blk + SUBLANE, :]
    qk_all = y * _sigmoid(y)

    pre = sm_ref[...] + gb_ref[...]
    f_cum = _chunk_cumsum(_log_sigmoid(pre), tri_ref[...])
    lane = lax.broadcasted_iota(jnp.int32, (t_blk, LANE), 1)
    is_f = jnp.logical_and(lane >= SM_MF, lane < SM_MF + N_HEADS)
    cols = jnp.where(is_f, f_cum, pre)
    rows_t = cols.T

    allowed = mask_ref[...] > 0.5
    n_chunks = t_blk // CHUNK
    chunk_rows = [slice(c * CHUNK, (c + 1) * CHUNK) for c in range(n_chunks)]
    row_chunk = lax.broadcasted_iota(jnp.int32, (t_blk, LANE), 0) // CHUNK
    lane_chunk = lax.broadcasted_iota(jnp.int32, (1, t_blk), 1) // CHUNK
    rep = _split_dot(cols, rep_ref[...])

    def per_chunk_rep(vals):
        out = jnp.broadcast_to(vals[-1], (t_blk, LANE))
        for c in range(n_chunks - 2, -1, -1):
            out = jnp.where(row_chunk == c, vals[c], out)
        return out

    def widen(x, width):
        return jnp.concatenate([x] * (width // LANE), axis=1) if width > LANE else x

    k_scale = QK_DIM ** -0.5
    heads = range(N_HEADS)
    vcs = [slice(h * V_DIM, (h + 1) * V_DIM) for h in heads]
    q = [qk_all[:, h * QK_DIM:(h + 1) * QK_DIM] for h in heads]
    k = [qk_all[:, QK_W + h * QK_DIM:QK_W + (h + 1) * QK_DIM] * k_scale for h in heads]
    qb = [x.astype(BF16) for x in q]
    v = [v_ref[:, vc].astype(BF16) for vc in vcs]
    li_rep = [rep[:, (2 * h) * LANE:(2 * h + 1) * LANE] for h in heads]
    f_rep = [rep[:, (2 * h + 1) * LANE:(2 * h + 2) * LANE] for h in heads]
    lir = [rows_t[SM_MI + h:SM_MI + h + 1, :] for h in heads]
    fr = [rows_t[SM_MF + h:SM_MF + h + 1, :] for h in heads]

    f_last, g, d_c, d_n, lmat, l_max, qk = [], [], [], [], [], [], []
    for h in heads:
        fl = [fr[h][:, (c + 1) * CHUNK - 1:(c + 1) * CHUNK] for c in range(n_chunks)]
        fl_row = fl[-1]
        for c in range(n_chunks - 2, -1, -1):
            fl_row = jnp.where(lane_chunk == c, fl[c], fl_row)
        g_row = fl_row - fr[h] + lir[h]
        gh = [jnp.max(jnp.where(lane_chunk == c, g_row, -jnp.inf), axis=-1, keepdims=True)
              for c in range(n_chunks)]
        fl_g = per_chunk_rep([a - b for a, b in zip(fl, gh)])
        kw = k[h] * jnp.exp(fl_g - f_rep[h] + li_rep[h])
        kwb = kw.astype(BF16)
        f_last.append(fl)
        g.append(gh)
        d_c.append([_dot_tn(kwb[r], v[h][r]) for r in chunk_rows])
        d_n.append([jnp.sum(kw[r], axis=0, keepdims=True) for r in chunk_rows])
        lm = jnp.where(allowed, widen(f_rep[h], t_blk) + (lir[h] - fr[h]), -jnp.inf)
        lmat.append(lm)
        l_max.append(jnp.max(lm, axis=-1, keepdims=True))
        qk.append(_dot_nt(qb[h], k[h].astype(BF16)))

    m_prev = [m_ref[h][:, 0:1] for h in heads]
    c_prev = [c_ref[h] for h in heads]
    n_prev = [n_ref[h] for h in heads]
    m_in = [[] for _ in heads]
    inter = [[] for _ in heads]
    qn = [[] for _ in heads]
    for c, r in enumerate(chunk_rows):
        for h in heads:
            m_in[h].append(m_prev[h])
            inter[h].append(jnp.dot(qb[h][r], c_prev[h].astype(BF16),
                                    preferred_element_type=F32))
            qn[h].append(jnp.sum(q[h][r] * n_prev[h], axis=-1, keepdims=True))
            m_new = jnp.maximum(f_last[h][c] + m_prev[h], g[h][c])
            a = jnp.exp(f_last[h][c] + m_prev[h] - m_new)
            bb = jnp.exp(g[h][c] - m_new)
            c_prev[h] = a * c_prev[h] + bb * d_c[h][c]
            n_prev[h] = a * n_prev[h] + bb * d_n[h][c]
            m_prev[h] = m_new
    for h in heads:
        c_ref[h] = c_prev[h]
        n_ref[h] = n_prev[h]
        m_ref[h] = jnp.broadcast_to(m_prev[h], (1, LANE))

    for h in heads:
        m_inter = f_rep[h] + per_chunk_rep(m_in[h])
        m = jnp.maximum(l_max[h], m_inter)
        s_mat = qk[h] * jnp.exp(lmat[h] - widen(m, t_blk))
        w_inter = jnp.exp(m_inter - m)
        num = jnp.dot(s_mat.astype(BF16), v[h], preferred_element_type=F32)
        num += widen(w_inter, V_DIM) * jnp.concatenate(inter[h], axis=0)
        den = jnp.sum(s_mat, axis=-1, keepdims=True)
        den = den + w_inter * jnp.concatenate(qn[h], axis=0)
        hid = num / widen(jnp.maximum(jnp.abs(den), jnp.exp(-m)), V_DIM)
        o = _sigmoid(mo_ref[:, vcs[h]]) * hid
        o_ref[:, vcs[h]] = _head_rmsnorm(o, mn_ref[:, vcs[h]]).astype(o_ref.dtype)


def mlstm_branch(z, tri, mask, rep_sel, conv_w, conv_b, gate_bias, mn, layer, *, t_blk):
    s = z.shape[0]
    return pl.pallas_call(
        functools.partial(_mlstm_kernel, t_blk=t_blk),
        grid=(s // t_blk,),
        in_specs=[
            pl.BlockSpec((t_blk, 2 * QK_W), lambda t: (t, _col_block(COL_MQK, 2 * QK_W))),
            pl.BlockSpec((t_blk, V_W), lambda t: (t, _col_block(COL_MV, V_W))),
            pl.BlockSpec((t_blk, V_W), lambda t: (t, _col_block(COL_MO, V_W))),
            pl.BlockSpec((t_blk, LANE), lambda t: (t, COL_SMALL)),
            pl.BlockSpec((t_blk, t_blk), lambda t: (0, 0)),
            pl.BlockSpec((t_blk, t_blk), lambda t: (0, 0)),
            pl.BlockSpec((LANE, 2 * N_HEADS * LANE), lambda t: (0, 0)),
            pl.BlockSpec((None, CONV_W, 2 * QK_W), lambda t: (layer, 0, 0)),
            pl.BlockSpec((None, 1, 2 * QK_W), lambda t: (layer, 0, 0)),
            pl.BlockSpec((None, 1, LANE), lambda t: (layer, 0, 0)),
            pl.BlockSpec((None, 1, V_W), lambda t: (layer, 0, 0)),
        ],
        out_specs=pl.BlockSpec((t_blk, V_W), lambda t: (t, 0)),
        out_shape=jax.ShapeDtypeStruct((s, V_W), BF16),
        scratch_shapes=[pltpu.VMEM((SUBLANE + t_blk, 2 * QK_W), F32),
                        pltpu.VMEM((N_HEADS, QK_DIM, V_DIM), F32),
                        pltpu.VMEM((N_HEADS, 1, QK_DIM), F32),
                        pltpu.VMEM((N_HEADS, 1, LANE), F32)],
        compiler_params=_cparams(("arbitrary",)),
        name="mlstm",
    )(z, z, z, z, tri, mask, rep_sel, conv_w, conv_b, gate_bias, mn)


def _memattn_kernel(q_ref, km_ref, vm_ref, o_ref):
    scale = V_DIM ** -0.5
    for h in range(N_HEADS):
        cols = slice(h * V_DIM, (h + 1) * V_DIM)
        q = q_ref[:, cols].astype(BF16)
        s = _dot_nt(q, km_ref[:, cols]) * scale
        e = jnp.exp(s - jnp.max(s, axis=-1, keepdims=True))
        p = e / jnp.sum(e, axis=-1, keepdims=True)
        o = jnp.dot(p.astype(BF16), vm_ref[:, cols], preferred_element_type=F32)
        o_ref[:, cols] = o.astype(o_ref.dtype)


def memattn_branch(z, km, vm, *, tm=TM_ATTN):
    s = z.shape[0]
    tm = min(tm, s)
    return pl.pallas_call(
        _memattn_kernel,
        grid=(s // tm,),
        in_specs=[pl.BlockSpec((tm, V_W), lambda i: (i, _col_block(COL_XQ, V_W))),
                  pl.BlockSpec((MEM_LEN, V_W), lambda i: (0, 0)),
                  pl.BlockSpec((MEM_LEN, V_W), lambda i: (0, 0))],
        out_specs=pl.BlockSpec((tm, V_W), lambda i: (i, 0)),
        out_shape=jax.ShapeDtypeStruct((s, V_W), BF16),
        compiler_params=_cparams(("parallel",)),
        name="memattn",
    )(z, km, vm)


def _merge_kernel(*refs, n_side):
    o1_ref, o2_ref, o3_ref, gl_ref, wb_ref, wg_ref, bg_ref = refs[:7]
    side_in = refs[7:7 + n_side]
    y_ref = refs[7 + n_side]
    side_out = refs[8 + n_side:8 + 2 * n_side]
    _side_cast(side_in, side_out)
    w = o1_ref.shape[1]
    gl = gl_ref[...].astype(BF16)
    for c0 in range(0, y_ref.shape[1], MERGE_STRIP):
        cols = slice(c0, c0 + MERGE_STRIP)
        y = None
        for j, o_ref in enumerate((o1_ref, o2_ref, o3_ref)):
            gate = _sigmoid(jnp.dot(gl, wg_ref[j * GATE_RANK:(j + 1) * GATE_RANK, cols],
                                    preferred_element_type=F32) + bg_ref[j:j + 1, cols])
            term = gate * jnp.dot(o_ref[...], wb_ref[j * w:(j + 1) * w, cols],
                                  preferred_element_type=F32)
            y = term if y is None else y + term
        y_ref[:, cols] = y.astype(y_ref.dtype)


def merge_branches(o1, o2, o3, z, wb, wg, bg, layer, *, tm=TM, tn=TN_MERGE, side_cast=()):
    s, w = o1.shape
    d = wb.shape[-1]
    tm = min(tm, s)
    grid = (d // tn, s // tm)
    o_spec = pl.BlockSpec((tm, w), lambda j, i: (i, 0))
    s_in, s_out, s_shapes = _side_cast_specs(side_cast, layer, grid[0] * grid[1],
                                             lambda j, i: j * grid[1] + i)
    return pl.pallas_call(
        functools.partial(_merge_kernel, n_side=len(side_cast)),
        grid=grid,
        in_specs=[o_spec, o_spec, o_spec,
                  pl.BlockSpec((tm, GATE_RANK), lambda j, i: (i, _col_block(COL_GATE, GATE_RANK))),
                  pl.BlockSpec((N_BRANCH * w, tn), lambda j, i: (0, j)),
                  pl.BlockSpec((N_BRANCH * GATE_RANK, tn), lambda j, i: (0, j)),
                  pl.BlockSpec((None, N_BRANCH, tn), lambda j, i: (layer, 0, j))] + s_in,
        out_specs=[pl.BlockSpec((tm, tn), lambda j, i: (i, j))] + s_out,
        out_shape=[jax.ShapeDtypeStruct((s, d), BF16)] + s_shapes,
        compiler_params=_cparams(("parallel", "parallel")),
        name="merge",
    )(o1, o2, o3, z, wb, wg, bg, *side_cast)


_IN_GROUPS = ("gq", "gk", "gv", "gg", "ga", "mqk", "mv", "mo", "mi", "mf", "xq", "gate")
_IN_SIZES = (QK_W, QK_W, V_W, V_W, GLA_RANK, 2 * QK_W, V_W, V_W, N_HEADS, N_HEADS, V_W, GATE_RANK)
_IN_SRC = {name: sum(_IN_SIZES[:i]) for i, name in enumerate(_IN_GROUPS)}
_IN_WIDTH = dict(zip(_IN_GROUPS, _IN_SIZES))
_IN_DST = {"gv": COL_GV, "gg": COL_GG, "mv": COL_MV, "mo": COL_MO, "xq": COL_XQ,
           "mqk": COL_MQK, "gq": COL_GQ, "gk": COL_GK, "gate": COL_GATE}
_IN_SMALL = ("ga", "mi", "mf")
assert (SM_GA, SM_MI, SM_MF) == (0, GLA_RANK, GLA_RANK + N_HEADS)


def _permute_wt_kernel(w_ref, o_ref):
    lanes = w_ref.shape[1]
    for name, col in _IN_DST.items():
        src, width = _IN_SRC[name], _IN_WIDTH[name]
        o_ref[col * LANE:col * LANE + width, :] = w_ref[src:src + width, :].astype(o_ref.dtype)
    spans = []
    for n in _IN_SMALL:
        if spans and spans[-1][0] + spans[-1][1] == _IN_SRC[n]:
            spans[-1] = (spans[-1][0], spans[-1][1] + _IN_WIDTH[n])
        else:
            spans.append((_IN_SRC[n], _IN_WIDTH[n]))
    small = [w_ref[src:src + width, :] for src, width in spans]
    used = sum(width for _, width in spans)
    small.append(jnp.zeros(((IN_COLS - COL_SMALL) * LANE - used, lanes), F32))
    o_ref[COL_SMALL * LANE:, :] = jnp.concatenate(small, axis=0).astype(o_ref.dtype)


def permute_w_in_t(w_in_t, *, tc=256):
    depth, n_src, d = w_in_t.shape
    n_dst = IN_COLS * LANE
    return pl.pallas_call(
        _permute_wt_kernel,
        grid=(depth, d // tc),
        in_specs=[pl.BlockSpec((None, n_src, tc), lambda l, i: (l, 0, i))],
        out_specs=pl.BlockSpec((None, n_dst, tc), lambda l, i: (l, 0, i)),
        out_shape=jax.ShapeDtypeStruct((depth, n_dst, d), BF16),
        compiler_params=_cparams(("parallel", "parallel")),
        name="permute_w_in",
    )(w_in_t)


def kernel(x, mem, norm_mix, w_in, w_gla_a, b_gla_a, gla_norm, conv_w, conv_b, b_ml_i, b_ml_f, ml_norm, mem_norm, w_mem_k, w_mem_v, w_branch, w_gate, b_gate, w_out, norm_ffn, w_ff1, w_ff2, final_norm):
    depth = w_in.shape[0]
    _, s, d = x.shape
    xs = x.reshape(s, d)
    t_blk = min(T_MIX, s)

    w_in_pt = permute_w_in_t(jnp.swapaxes(w_in, 1, 2))
    wa_pad = jnp.zeros((depth, LANE, QK_W), F32)
    wa_pad = wa_pad.at[:, SM_GA:SM_GA + GLA_RANK, :].set(w_gla_a).astype(BF16)
    gate_bias = jnp.zeros((depth, 1, LANE), F32)
    gate_bias = gate_bias.at[:, 0, SM_MI:SM_MI + N_HEADS].set(b_ml_i)
    gate_bias = gate_bias.at[:, 0, SM_MF:SM_MF + N_HEADS].set(b_ml_f)
    tri = chunk_tri(t_blk)
    tri_f = tri.astype(F32)
    rep_sel = gate_replicator()
    b_gla_a3 = b_gla_a.reshape(depth, 1, QK_W)
    gla_norm3 = gla_norm.reshape(depth, 1, V_W)
    conv_b3 = conv_b.reshape(depth, 1, 2 * QK_W)
    ml_norm3 = ml_norm.reshape(depth, 1, V_W)
    w_branch2 = w_branch.reshape(depth, N_BRANCH * V_W, d)
    w_gate2 = w_gate.reshape(depth, N_BRANCH * GATE_RANK, d)

    memn = rmsnorm(mem.reshape(MEM_LEN, d), mem_norm, BF16, tm=MEM_LEN)

    for l in range(depth):
        z, wb_b, wg_b, w_out_b = norm_matmul_rows(
            xs, norm_mix[l], w_in_pt, l, tm=TM, tn=TN_IN, out_dtype=F32, transposed_b=True,
            side_cast=(w_branch2, w_gate2, w_out), name="in_proj")
        km = matmul_cols(memn, w_mem_k, l, tm=MEM_LEN, tn=TN_MEM, out_dtype=BF16, name="mem_k")
        vm = matmul_cols(memn, w_mem_v, l, tm=MEM_LEN, tn=TN_MEM, out_dtype=BF16, name="mem_v")
        o_gla = gla_branch(z, tri, tri_f, wa_pad, b_gla_a3, gla_norm3, l, t_blk=t_blk)
        o_ml = mlstm_branch(z, tri, tri_f, rep_sel, conv_w, conv_b3, gate_bias, ml_norm3, l, t_blk=t_blk)
        o_mem = memattn_branch(z, km, vm)
        y, w_ff1_b = merge_branches(o_gla, o_ml, o_mem, z, wb_b, wg_b, b_gate, l,
                                    side_cast=(w_ff1,))
        xs = matmul_acc(y, w_out_b, l, xs, tm=TM, tn=TN_OUT, tk=d, name="out_proj")
        hid, w_ff2_b = norm_matmul_rows(xs, norm_ffn[l], w_ff1_b, l, tm=TM, tn=TN_FFN,
                                        out_dtype=BF16, epilogue="relu2", side_cast=(w_ff2,),
                                        name="ffn_up")
        xs = matmul_acc(hid, w_ff2_b, l, xs, tm=TM, tn=TN_FFN, tk=TK_FFN_DOWN, name="ffn_down")
    return rmsnorm(xs, final_norm, F32).reshape(x.shape)
```

```python
import functools

import jax
import jax.numpy as jnp
from jax import lax
from jax.experimental import pallas as pl
from jax.experimental.pallas import tpu as pltpu

F32 = jnp.float32
BF16 = jnp.bfloat16

EPS = 1e-6
CHUNK = 64
N_HEADS = 4
QK_DIM = 128
V_DIM = 256
QK_W = N_HEADS * QK_DIM
V_W = N_HEADS * V_DIM
GLA_RANK = 16
GLA_TAU = 16.0
CONV_W = 4
MEM_LEN = 256
GATE_RANK = 256
N_BRANCH = 3

LANE = 128
SUBLANE = 8
BF16_SUBLANES = 16
VMEM_LIMIT = 56 * 1024 * 1024

TM = 1024
TN_IN, TN_MERGE, TN_OUT, TN_FFN = 768, 1024, 1024, 1024
TK_FFN_DOWN = 4096
T_MIX = 4 * CHUNK
TM_ATTN = 1024
TM_NORM = 512
TN_MEM = 256
MERGE_STRIP = 512

COL_GV, COL_GG, COL_MV, COL_MO, COL_XQ = 0, 8, 16, 24, 32
COL_MQK, COL_GQ, COL_GK = 40, 48, 52
COL_GATE, COL_SMALL, IN_COLS = 56, 58, 60
SM_GA, SM_MI, SM_MF = 0, 16, 20


def _cparams(sem):
    return pltpu.CompilerParams(dimension_semantics=sem, vmem_limit_bytes=VMEM_LIMIT)


def _col_block(col, width):
    return col * LANE // width


def _rmsnorm_kernel(x_ref, g_ref, o_ref):
    x = x_ref[...]
    ms = jnp.mean(x * x, axis=-1, keepdims=True)
    o_ref[...] = ((x * lax.rsqrt(ms + EPS)) * g_ref[...]).astype(o_ref.dtype)


def rmsnorm(x, g, out_dtype, tm=TM_NORM):
    m, d = x.shape
    tm = min(tm, m)
    return pl.pallas_call(
        _rmsnorm_kernel,
        grid=(m // tm,),
        in_specs=[pl.BlockSpec((tm, d), lambda i: (i, 0)),
                  pl.BlockSpec((1, d), lambda i: (0, 0))],
        out_specs=pl.BlockSpec((tm, d), lambda i: (i, 0)),
        out_shape=jax.ShapeDtypeStruct((m, d), out_dtype),
        compiler_params=_cparams(("parallel",)),
        name="rmsnorm",
    )(x, g.reshape(1, d))


def _apply_epilogue(acc, epilogue):
    if epilogue == "relu2":
        return jnp.square(jnp.maximum(acc, 0.0))
    return acc


def _side_cast_specs(srcs, layer, n_steps, step_of):
    in_specs, out_specs, out_shapes = [], [], []
    for w in srcs:
        r, c = w.shape[-2:]
        n_blocks = 1 << (n_steps.bit_length() - 1)
        while r % n_blocks or (r // n_blocks) % BF16_SUBLANES:
            n_blocks //= 2
        rows = r // n_blocks

        def block(*g, last=n_blocks - 1):
            return jnp.minimum(step_of(*g), last)

        in_specs.append(pl.BlockSpec((None, rows, c), lambda *g, b=block: (layer, b(*g), 0)))
        out_specs.append(pl.BlockSpec((rows, c), lambda *g, b=block: (b(*g), 0)))
        out_shapes.append(jax.ShapeDtypeStruct((r, c), BF16))
    return in_specs, out_specs, out_shapes


def _side_cast(side_in, side_out):
    for src_ref, dst_ref in zip(side_in, side_out):
        dst_ref[...] = src_ref[...].astype(dst_ref.dtype)


def _weight_spec(b, layer, block, index):
    if b.ndim == 2:
        return pl.BlockSpec(block, index)
    return pl.BlockSpec((None,) + block, lambda *g: (layer,) + index(*g))


def _norm_mm_rows_kernel(*refs, transposed_b, epilogue, n_side, ts, n_slab):
    x_ref, g_ref, b_ref = refs[:3]
    side_in = refs[3:3 + n_side]
    o_ref = refs[3 + n_side]
    side_out = refs[4 + n_side:4 + 2 * n_side]
    h_even, h_odd = refs[4 + 2 * n_side:]
    r = pl.program_id(0)
    slab = jnp.minimum(pl.program_id(1), n_slab - 1)
    dims = (((1,), (1 if transposed_b else 0,)), ((), ()))

    def step(read_ref, write_ref):
        _side_cast(side_in, side_out)
        if read_ref is not None:
            acc = lax.dot_general(read_ref[...], b_ref[...], dims, preferred_element_type=F32)
            o_ref[...] = _apply_epilogue(acc, epilogue).astype(o_ref.dtype)
        x = x_ref[...]
        ms = jnp.mean(x * x, axis=-1, keepdims=True)
        h = ((x * lax.rsqrt(ms + EPS)) * g_ref[...]).astype(write_ref.dtype)
        write_ref[pl.ds(pl.multiple_of(slab * ts, ts), ts), :] = h

    @pl.when(r == 0)
    def _():
        step(None, h_even)

    @pl.when(r % 2 == 1)
    def _():
        step(h_even, h_odd)

    @pl.when(jnp.logical_and(r > 0, r % 2 == 0))
    def _():
        step(h_odd, h_even)


def norm_matmul_rows(x, g, b, layer, *, tm, tn, out_dtype, name, epilogue=None,
                     transposed_b=False, side_cast=()):
    m, kdim = x.shape
    n = b.shape[-2] if transposed_b else b.shape[-1]
    tm = min(tm, m)
    ni, nj = m // tm, n // tn
    n_slab = 1 << (nj.bit_length() - 1)
    ts = tm // n_slab
    def col(r, j):
        return jnp.where(r == 0, 0, j)

    b_spec = (_weight_spec(b, layer, (tn, kdim), lambda r, j: (col(r, j), 0)) if transposed_b
              else _weight_spec(b, layer, (kdim, tn), lambda r, j: (0, col(r, j))))
    s_in, s_out, s_shapes = _side_cast_specs(
        side_cast, layer, ni * nj, lambda r, j: jnp.where(r == 0, 0, (r - 1) * nj + j))
    x_spec = pl.BlockSpec(
        (ts, kdim),
        lambda r, j: (jnp.minimum(r, ni - 1) * n_slab + jnp.minimum(j, n_slab - 1), 0))
    return pl.pallas_call(
        functools.partial(_norm_mm_rows_kernel, transposed_b=transposed_b, epilogue=epilogue,
                          n_side=len(side_cast), ts=ts, n_slab=n_slab),
        grid=(ni + 1, nj),
        in_specs=[x_spec, pl.BlockSpec((1, kdim), lambda r, j: (0, 0)), b_spec] + s_in,
        out_specs=[pl.BlockSpec((tm, tn), lambda r, j: (jnp.maximum(r - 1, 0), col(r, j)))] + s_out,
        out_shape=[jax.ShapeDtypeStruct((m, n), out_dtype)] + s_shapes,
        scratch_shapes=[pltpu.VMEM((tm, kdim), BF16), pltpu.VMEM((tm, kdim), BF16)],
        compiler_params=_cparams(("arbitrary", "arbitrary")),
        name=name,
    )(x, g.reshape(1, kdim), b, *side_cast)


def _mm_cols_kernel(a_ref, b_ref, o_ref, wb_ref, *, epilogue):
    @pl.when(pl.program_id(1) == 0)
    def _():
        wb_ref[...] = b_ref[...].astype(BF16)

    acc = jnp.dot(a_ref[...].astype(BF16), wb_ref[...], preferred_element_type=F32)
    o_ref[...] = _apply_epilogue(acc, epilogue).astype(o_ref.dtype)


def matmul_cols(a, b, layer, *, tm, tn, out_dtype, epilogue=None, name):
    m, kdim = a.shape
    n = b.shape[-1]
    tm = min(tm, m)
    return pl.pallas_call(
        functools.partial(_mm_cols_kernel, epilogue=epilogue),
        grid=(n // tn, m // tm),
        in_specs=[pl.BlockSpec((tm, kdim), lambda j, i: (i, 0)),
                  pl.BlockSpec((None, kdim, tn), lambda j, i: (layer, 0, j))],
        out_specs=pl.BlockSpec((tm, tn), lambda j, i: (i, j)),
        out_shape=jax.ShapeDtypeStruct((m, n), out_dtype),
        scratch_shapes=[pltpu.VMEM((kdim, tn), BF16)],
        compiler_params=_cparams(("parallel", "arbitrary")),
        name=name,
    )(a, b)


def _mm_acc_kernel(a_ref, b_ref, r_ref, o_ref, *, nk):
    def partial_sum():
        return jnp.dot(a_ref[...], b_ref[...], preferred_element_type=F32)

    if nk == 1:
        o_ref[...] = partial_sum() + r_ref[...]
        return
    k = pl.program_id(2)

    @pl.when(k == 0)
    def _():
        o_ref[...] = partial_sum()

    @pl.when(jnp.logical_and(k > 0, k < nk - 1))
    def _():
        o_ref[...] = partial_sum() + o_ref[...]

    @pl.when(k == nk - 1)
    def _():
        o_ref[...] = (partial_sum() + o_ref[...]) + r_ref[...]


def matmul_acc(a, b, layer, residual, *, tm, tn, tk, name):
    m, kdim = a.shape
    n = b.shape[-1]
    tm = min(tm, m)
    return pl.pallas_call(
        functools.partial(_mm_acc_kernel, nk=kdim // tk),
        grid=(m // tm, n // tn, kdim // tk),
        in_specs=[pl.BlockSpec((tm, tk), lambda i, j, k: (i, k)),
                  _weight_spec(b, layer, (tk, tn), lambda i, j, k: (k, j)),
                  pl.BlockSpec((tm, tn), lambda i, j, k: (i, j))],
        out_specs=pl.BlockSpec((tm, tn), lambda i, j, k: (i, j)),
        out_shape=jax.ShapeDtypeStruct((m, n), F32),
        compiler_params=_cparams(("parallel", "parallel", "arbitrary")),
        name=name,
    )(a, b, residual)


def _log_sigmoid(x):
    return jnp.minimum(x, 0.0) - jnp.log1p(jnp.exp(-jnp.abs(x)))


def _sigmoid(x):
    return jax.nn.sigmoid(x)


def chunk_tri(t):
    r = lax.broadcasted_iota(jnp.int32, (t, t), 0)
    c = lax.broadcasted_iota(jnp.int32, (t, t), 1)
    same = (r // CHUNK) == (c // CHUNK)
    return jnp.logical_and(same, c <= r).astype(BF16)


def _chunk_cumsum(x, tri):
    hi = x.astype(BF16)
    r1 = x - hi.astype(F32)
    mid = r1.astype(BF16)
    lo = (r1 - mid.astype(F32)).astype(BF16)
    out = jnp.dot(tri, hi, preferred_element_type=F32)
    out += jnp.dot(tri, mid, preferred_element_type=F32)
    out += jnp.dot(tri, lo, preferred_element_type=F32)
    return out


def _split_dot(x, sel):
    hi = x.astype(BF16)
    r1 = x - hi.astype(F32)
    mid = r1.astype(BF16)
    lo = (r1 - mid.astype(F32)).astype(BF16)
    out = jnp.dot(hi, sel, preferred_element_type=F32)
    out += jnp.dot(mid, sel, preferred_element_type=F32)
    out += jnp.dot(lo, sel, preferred_element_type=F32)
    return out


def gate_replicator():
    src = lax.broadcasted_iota(jnp.int32, (LANE, 2 * N_HEADS * LANE), 0)
    grp = lax.broadcasted_iota(jnp.int32, (LANE, 2 * N_HEADS * LANE), 1) // LANE
    want = jnp.where(grp % 2 == 0, SM_MI, SM_MF) + grp // 2
    return (src == want).astype(BF16)


def _dot_nt(a, b):
    return lax.dot_general(a, b, (((1,), (1,)), ((), ())), preferred_element_type=F32)


def _dot_tn(a, b):
    return lax.dot_general(a, b, (((0,), (0,)), ((), ())), preferred_element_type=F32)


def _head_rmsnorm(o, g):
    ms = jnp.mean(o * o, axis=-1, keepdims=True)
    return (o * lax.rsqrt(ms + EPS)) * g


def _gla_kernel(q_ref, k_ref, v_ref, gg_ref, sm_ref, tri_ref, mask_ref, wa_ref, ba_ref, gn_ref,
                o_ref, st_ref, *, t_blk):
    @pl.when(pl.program_id(0) == 0)
    def _():
        st_ref[...] = jnp.zeros_like(st_ref)

    pre = jnp.dot(sm_ref[...].astype(BF16), wa_ref[...],
                  preferred_element_type=F32) + ba_ref[...]
    log_a = _log_sigmoid(pre) / GLA_TAU
    b_all = _chunk_cumsum(log_a, tri_ref[...])
    allowed = mask_ref[...] > 0.5
    scale = QK_DIM ** -0.5
    n_chunks = t_blk // CHUNK
    chunk_rows = [slice(c * CHUNK, (c + 1) * CHUNK) for c in range(n_chunks)]
    heads = range(N_HEADS)
    vcs = [slice(h * V_DIM, (h + 1) * V_DIM) for h in heads]

    q_d, o_intra, d_st, decay = [], [], [], []
    for h in heads:
        kc = slice(h * QK_DIM, (h + 1) * QK_DIM)
        b = b_all[:, kc]
        b_last = [b[r][CHUNK - 1:CHUNK] for r in chunk_rows]
        b_last_rows = jnp.concatenate(
            [jnp.broadcast_to(bl, (CHUNK, QK_DIM)) for bl in b_last], axis=0)
        k = k_ref[:, kc]
        v = v_ref[:, vcs[h]].astype(BF16)
        qd = ((q_ref[:, kc] * scale) * jnp.exp(b)).astype(BF16)
        k_d = (k * jnp.exp(-b)).astype(BF16)
        k_s = (k * jnp.exp(b_last_rows - b)).astype(BF16)
        att = jnp.where(allowed, _dot_nt(qd, k_d), 0.0)
        q_d.append(qd)
        o_intra.append(jnp.dot(att.astype(BF16), v, preferred_element_type=F32))
        d_st.append([_dot_tn(v[r], k_s[r]) for r in chunk_rows])
        decay.append([jnp.exp(bl) for bl in b_last])

    states = [st_ref[h] for h in heads]
    inter = [[] for _ in heads]
    for c, r in enumerate(chunk_rows):
        for h in heads:
            inter[h].append(_dot_nt(q_d[h][r], states[h].astype(BF16)))
            states[h] = states[h] * decay[h][c] + d_st[h][c]
    for h in heads:
        st_ref[h] = states[h]

    for h in heads:
        o = o_intra[h] + jnp.concatenate(inter[h], axis=0)
        gg = gg_ref[:, vcs[h]]
        o = _head_rmsnorm(o, gn_ref[:, vcs[h]]) * (gg * _sigmoid(gg))
        o_ref[:, vcs[h]] = o.astype(o_ref.dtype)


def gla_branch(z, tri, mask, wa_pad, ba, gn, layer, *, t_blk):
    s = z.shape[0]
    return pl.pallas_call(
        functools.partial(_gla_kernel, t_blk=t_blk),
        grid=(s // t_blk,),
        in_specs=[
            pl.BlockSpec((t_blk, QK_W), lambda t: (t, _col_block(COL_GQ, QK_W))),
            pl.BlockSpec((t_blk, QK_W), lambda t: (t, _col_block(COL_GK, QK_W))),
            pl.BlockSpec((t_blk, V_W), lambda t: (t, _col_block(COL_GV, V_W))),
            pl.BlockSpec((t_blk, V_W), lambda t: (t, _col_block(COL_GG, V_W))),
            pl.BlockSpec((t_blk, LANE), lambda t: (t, COL_SMALL)),
            pl.BlockSpec((t_blk, t_blk), lambda t: (0, 0)),
            pl.BlockSpec((t_blk, t_blk), lambda t: (0, 0)),
            pl.BlockSpec((None, LANE, QK_W), lambda t: (layer, 0, 0)),
            pl.BlockSpec((None, 1, QK_W), lambda t: (layer, 0, 0)),
            pl.BlockSpec((None, 1, V_W), lambda t: (layer, 0, 0)),
        ],
        out_specs=pl.BlockSpec((t_blk, V_W), lambda t: (t, 0)),
        out_shape=jax.ShapeDtypeStruct((s, V_W), BF16),
        scratch_shapes=[pltpu.VMEM((N_HEADS, V_DIM, QK_DIM), F32)],
        compiler_params=_cparams(("arbitrary",)),
        name="gla",
    )(z, z, z, z, z, tri, mask, wa_pad, ba, gn)


def _mlstm_kernel(qk_ref, v_ref, mo_ref, sm_ref, tri_ref, mask_ref, rep_ref, cw_ref, cb_ref, gb_ref, mn_ref,
                  o_ref, u_ref, c_ref, n_ref, m_ref, *, t_blk):
    @pl.when(pl.program_id(0) == 0)
    def _():
        u_ref[0:SUBLANE, :] = jnp.zeros((SUBLANE, 2 * QK_W), F32)
        c_ref[...] = jnp.zeros_like(c_ref)
        n_ref[...] = jnp.zeros_like(n_ref)
        m_ref[...] = jnp.full(m_ref.shape, -1e30, F32)

    u_ref[SUBLANE:SUBLANE + t_blk, :] = qk_ref[...]
    y = cb_ref[...]
    for j in range(CONV_W):
        off = SUBLANE - (CONV_W - 1) + j
        y = y + cw_ref[j:j + 1, :] * u_ref[off:off + t_blk, :]
    u_ref[0:SUBLANE, :] = u_ref[t_blk:t_blk + SUBLANE, :]
    qk_all = y * _sigmoid(y)

    pre = sm_ref[...] + gb_ref[...]
    f_cum = _chunk_cumsum(_log_sigmoid(pre), tri_ref[...])
    lane = lax.broadcasted_iota(jnp.int32, (t_blk, LANE), 1)
    is_f = jnp.logical_and(lane >= SM_MF, lane < SM_MF + N_HEADS)
    cols = jnp.where(is_f, f_cum, pre)
    rows_t = cols.T

    allowed = mask_ref[...] > 0.5
    n_chunks = t_blk // CHUNK
    chunk_rows = [slice(c * CHUNK, (c + 1) * CHUNK) for c in range(n_chunks)]
    row_chunk = lax.broadcasted_iota(jnp.int32, (t_blk, LANE), 0) // CHUNK
    lane_chunk = lax.broadcasted_iota(jnp.int32, (1, t_blk), 1) // CHUNK
    rep = _split_dot(cols, rep_ref[...])

    def per_chunk_rep(vals):
        out = jnp.broadcast_to(vals[-1], (t_blk, LANE))
        for c in range(n_chunks - 2, -1, -1):
            out = jnp.where(row_chunk == c, vals[c], out)
        return out

    def widen(x, width):
        return jnp.concatenate([x] * (width // LANE), axis=1) if width > LANE else x

    k_scale = QK_DIM ** -0.5
    heads = range(N_HEADS)
    vcs = [slice(h * V_DIM, (h + 1) * V_DIM) for h in heads]
    q = [qk_all[:, h * QK_DIM:(h + 1) * QK_DIM] for h in heads]
    k = [qk_all[:, QK_W + h * QK_DIM:QK_W + (h + 1) * QK_DIM] * k_scale for h in heads]
    qb = [x.astype(BF16) for x in q]
    v = [v_ref[:, vc].astype(BF16) for vc in vcs]
    li_rep = [rep[:, (2 * h) * LANE:(2 * h + 1) * LANE] for h in heads]
    f_rep = [rep[:, (2 * h + 1) * LANE:(2 * h + 2) * LANE] for h in heads]
    lir = [rows_t[SM_MI + h:SM_MI + h + 1, :] for h in heads]
    fr = [rows_t[SM_MF + h:SM_MF + h + 1, :] for h in heads]

    f_last, g, d_c, d_n, lmat, l_max, qk = [], [], [], [], [], [], []
    for h in heads:
        fl = [fr[h][:, (c + 1) * CHUNK - 1:(c + 1) * CHUNK] for c in range(n_chunks)]
        fl_row = fl[-1]
        for c in range(n_chunks - 2, -1, -1):
            fl_row = jnp.where(lane_chunk == c, fl[c], fl_row)
        g_row = fl_row - fr[h] + lir[h]
        gh = [jnp.max(jnp.where(lane_chunk == c, g_row, -jnp.inf), axis=-1, keepdims=True)
              for c in range(n_chunks)]
        fl_g = per_chunk_rep([a - b for a, b in zip(fl, gh)])
        kw = k[h] * jnp.exp(fl_g - f_rep[h] + li_rep[h])
        kwb = kw.astype(BF16)
        f_last.append(fl)
        g.append(gh)
        d_c.append([_dot_tn(kwb[r], v[h][r]) for r in chunk_rows])
        d_n.append([jnp.sum(kw[r], axis=0, keepdims=True) for r in chunk_rows])
        lm = jnp.where(allowed, widen(f_rep[h], t_blk) + (lir[h] - fr[h]), -jnp.inf)
        lmat.append(lm)
        l_max.append(jnp.max(lm, axis=-1, keepdims=True))
        qk.append(_dot_nt(qb[h], k[h].astype(BF16)))

    m_prev = [m_ref[h][:, 0:1] for h in heads]
    c_prev = [c_ref[h] for h in heads]
    n_prev = [n_ref[h] for h in heads]
    m_in = [[] for _ in heads]
    inter = [[] for _ in heads]
    qn = [[] for _ in heads]
    for c, r in enumerate(chunk_rows):
        for h in heads:
            m_in[h].append(m_prev[h])
            inter[h].append(jnp.dot(qb[h][r], c_prev[h].astype(BF16),
                                    preferred_element_type=F32))
            qn[h].append(jnp.sum(q[h][r] * n_prev[h], axis=-1, keepdims=True))
            m_new = jnp.maximum(f_last[h][c] + m_prev[h], g[h][c])
            a = jnp.exp(f_last[h][c] + m_prev[h] - m_new)
            bb = jnp.exp(g[h][c] - m_new)
            c_prev[h] = a * c_prev[h] + bb * d_c[h][c]
            n_prev[h] = a * n_prev[h] + bb * d_n[h][c]
            m_prev[h] = m_new
    for h in heads:
        c_ref[h] = c_prev[h]
        n_ref[h] = n_prev[h]
        m_ref[h] = jnp.broadcast_to(m_prev[h], (1, LANE))

    for h in heads:
        m_inter = f_rep[h] + per_chunk_rep(m_in[h])
        m = jnp.maximum(l_max[h], m_inter)
        s_mat = qk[h] * jnp.exp(lmat[h] - widen(m, t_blk))
        w_inter = jnp.exp(m_inter - m)
        num = jnp.dot(s_mat.astype(BF16), v[h], preferred_element_type=F32)
        num += widen(w_inter, V_DIM) * jnp.concatenate(inter[h], axis=0)
        den = jnp.sum(s_mat, axis=-1, keepdims=True)
        den = den + w_inter * jnp.concatenate(qn[h], axis=0)
        hid = num / widen(jnp.maximum(jnp.abs(den), jnp.exp(-m)), V_DIM)
        o = _sigmoid(mo_ref[:, vcs[h]]) * hid
        o_ref[:, vcs[h]] = _head_rmsnorm(o, mn_ref[:, vcs[h]]).astype(o_ref.dtype)


def mlstm_branch(z, tri, mask, rep_sel, conv_w, conv_b, gate_bias, mn, layer, *, t_blk):
    s = z.shape[0]
    return pl.pallas_call(
        functools.partial(_mlstm_kernel, t_blk=t_blk),
        grid=(s // t_blk,),
        in_specs=[
            pl.BlockSpec((t_blk, 2 * QK_W), lambda t: (t, _col_block(COL_MQK, 2 * QK_W))),
            pl.BlockSpec((t_blk, V_W), lambda t: (t, _col_block(COL_MV, V_W))),
            pl.BlockSpec((t_blk, V_W), lambda t: (t, _col_block(COL_MO, V_W))),
            pl.BlockSpec((t_blk, LANE), lambda t: (t, COL_SMALL)),
            pl.BlockSpec((t_blk, t_blk), lambda t: (0, 0)),
            pl.BlockSpec((t_blk, t_blk), lambda t: (0, 0)),
            pl.BlockSpec((LANE, 2 * N_HEADS * LANE), lambda t: (0, 0)),
            pl.BlockSpec((None, CONV_W, 2 * QK_W), lambda t: (layer, 0, 0)),
            pl.BlockSpec((None, 1, 2 * QK_W), lambda t: (layer, 0, 0)),
            pl.BlockSpec((None, 1, LANE), lambda t: (layer, 0, 0)),
            pl.BlockSpec((None, 1, V_W), lambda t: (layer, 0, 0)),
        ],
        out_specs=pl.BlockSpec((t_blk, V_W), lambda t: (t, 0)),
        out_shape=jax.ShapeDtypeStruct((s, V_W), BF16),
        scratch_shapes=[pltpu.VMEM((SUBLANE + t_blk, 2 * QK_W), F32),
                        pltpu.VMEM((N_HEADS, QK_DIM, V_DIM), F32),
                        pltpu.VMEM((N_HEADS, 1, QK_DIM), F32),
                        pltpu.VMEM((N_HEADS, 1, LANE), F32)],
        compiler_params=_cparams(("arbitrary",)),
        name="mlstm",
    )(z, z, z, z, tri, mask, rep_sel, conv_w, conv_b, gate_bias, mn)


def _memattn_kernel(q_ref, km_ref, vm_ref, o_ref):
    scale = V_DIM ** -0.5
    for h in range(N_HEADS):
        cols = slice(h * V_DIM, (h + 1) * V_DIM)
        q = q_ref[:, cols].astype(BF16)
        s = _dot_nt(q, km_ref[:, cols]) * scale
        e = jnp.exp(s - jnp.max(s, axis=-1, keepdims=True))
        p = e / jnp.sum(e, axis=-1, keepdims=True)
        o = jnp.dot(p.astype(BF16), vm_ref[:, cols], preferred_element_type=F32)
        o_ref[:, cols] = o.astype(o_ref.dtype)


def memattn_branch(z, km, vm, *, tm=TM_ATTN):
    s = z.shape[0]
    tm = min(tm, s)
    return pl.pallas_call(
        _memattn_kernel,
        grid=(s // tm,),
        in_specs=[pl.BlockSpec((tm, V_W), lambda i: (i, _col_block(COL_XQ, V_W))),
                  pl.BlockSpec((MEM_LEN, V_W), lambda i: (0, 0)),
                  pl.BlockSpec((MEM_LEN, V_W), lambda i: (0, 0))],
        out_specs=pl.BlockSpec((tm, V_W), lambda i: (i, 0)),
        out_shape=jax.ShapeDtypeStruct((s, V_W), BF16),
        compiler_params=_cparams(("parallel",)),
        name="memattn",
    )(z, km, vm)


def _merge_kernel(*refs, n_side):
    o1_ref, o2_ref, o3_ref, gl_ref, wb_ref, wg_ref, bg_ref = refs[:7]
    side_in = refs[7:7 + n_side]
    y_ref = refs[7 + n_side]
    side_out = refs[8 + n_side:8 + 2 * n_side]
    _side_cast(side_in, side_out)
    w = o1_ref.shape[1]
    gl = gl_ref[...].astype(BF16)
    for c0 in range(0, y_ref.shape[1], MERGE_STRIP):
        cols = slice(c0, c0 + MERGE_STRIP)
        y = None
        for j, o_ref in enumerate((o1_ref, o2_ref, o3_ref)):
            gate = _sigmoid(jnp.dot(gl, wg_ref[j * GATE_RANK:(j + 1) * GATE_RANK, cols],
                                    preferred_element_type=F32) + bg_ref[j:j + 1, cols])
            term = gate * jnp.dot(o_ref[...], wb_ref[j * w:(j + 1) * w, cols],
                                  preferred_element_type=F32)
            y = term if y is None else y + term
        y_ref[:, cols] = y.astype(y_ref.dtype)


def merge_branches(o1, o2, o3, z, wb, wg, bg, layer, *, tm=TM, tn=TN_MERGE, side_cast=()):
    s, w = o1.shape
    d = wb.shape[-1]
    tm = min(tm, s)
    grid = (d // tn, s // tm)
    o_spec = pl.BlockSpec((tm, w), lambda j, i: (i, 0))
    s_in, s_out, s_shapes = _side_cast_specs(side_cast, layer, grid[0] * grid[1],
                                             lambda j, i: j * grid[1] + i)
    return pl.pallas_call(
        functools.partial(_merge_kernel, n_side=len(side_cast)),
        grid=grid,
        in_specs=[o_spec, o_spec, o_spec,
                  pl.BlockSpec((tm, GATE_RANK), lambda j, i: (i, _col_block(COL_GATE, GATE_RANK))),
                  pl.BlockSpec((N_BRANCH * w, tn), lambda j, i: (0, j)),
                  pl.BlockSpec((N_BRANCH * GATE_RANK, tn), lambda j, i: (0, j)),
                  pl.BlockSpec((None, N_BRANCH, tn), lambda j, i: (layer, 0, j))] + s_in,
        out_specs=[pl.BlockSpec((tm, tn), lambda j, i: (i, j))] + s_out,
        out_shape=[jax.ShapeDtypeStruct((s, d), BF16)] + s_shapes,
        compiler_params=_cparams(("parallel", "parallel")),
        name="merge",
    )(o1, o2, o3, z, wb, wg, bg, *side_cast)


_IN_GROUPS = ("gq", "gk", "gv", "gg", "ga", "mqk", "mv", "mo", "mi", "mf", "xq", "gate")
_IN_SIZES = (QK_W, QK_W, V_W, V_W, GLA_RANK, 2 * QK_W, V_W, V_W, N_HEADS, N_HEADS, V_W, GATE_RANK)
_IN_SRC = {name: sum(_IN_SIZES[:i]) for i, name in enumerate(_IN_GROUPS)}
_IN_WIDTH = dict(zip(_IN_GROUPS, _IN_SIZES))
_IN_DST = {"gv": COL_GV, "gg": COL_GG, "mv": COL_MV, "mo": COL_MO, "xq": COL_XQ,
           "mqk": COL_MQK, "gq": COL_GQ, "gk": COL_GK, "gate": COL_GATE}
_IN_SMALL = ("ga", "mi", "mf")
assert (SM_GA, SM_MI, SM_MF) == (0, GLA_RANK, GLA_RANK + N_HEADS)


def _permute_wt_kernel(w_ref, o_ref):
    lanes = w_ref.shape[1]
    for name, col in _IN_DST.items():
        src, width = _IN_SRC[name], _IN_WIDTH[name]
        o_ref[col * LANE:col * LANE + width, :] = w_ref[src:src + width, :].astype(o_ref.dtype)
    spans = []
    for n in _IN_SMALL:
        if spans and spans[-1][0] + spans[-1][1] == _IN_SRC[n]:
            spans[-1] = (spans[-1][0], spans[-1][1] + _IN_WIDTH[n])
        else:
            spans.append((_IN_SRC[n], _IN_WIDTH[n]))
    small = [w_ref[src:src + width, :] for src, width in spans]
    used = sum(width for _, width in spans)
    small.append(jnp.zeros(((IN_COLS - COL_SMALL) * LANE - used, lanes), F32))
    o_ref[COL_SMALL * LANE:, :] = jnp.concatenate(small, axis=0).astype(o_ref.dtype)


def permute_w_in_t(w_in_t, *, tc=256):
    depth, n_src, d = w_in_t.shape
    n_dst = IN_COLS * LANE
    return pl.pallas_call(
        _permute_wt_kernel,
        grid=(depth, d // tc),
        in_specs=[pl.BlockSpec((None, n_src, tc), lambda l, i: (l, 0, i))],
        out_specs=pl.BlockSpec((None, n_dst, tc), lambda l, i: (l, 0, i)),
        out_shape=jax.ShapeDtypeStruct((depth, n_dst, d), BF16),
        compiler_params=_cparams(("parallel", "parallel")),
        name="permute_w_in",
    )(w_in_t)


def kernel(x, mem, norm_mix, w_in, w_gla_a, b_gla_a, gla_norm, conv_w, conv_b, b_ml_i, b_ml_f, ml_norm, mem_norm, w_mem_k, w_mem_v, w_branch, w_gate, b_gate, w_out, norm_ffn, w_ff1, w_ff2, final_norm):
    depth = w_in.shape[0]
    _, s, d = x.shape
    xs = x.reshape(s, d)
    t_blk = min(T_MIX, s)

    w_in_pt = permute_w_in_t(jnp.swapaxes(w_in, 1, 2))
    wa_pad = jnp.zeros((depth, LANE, QK_W), F32)
    wa_pad = wa_pad.at[:, SM_GA:SM_GA + GLA_RANK, :].set(w_gla_a).astype(BF16)
    gate_bias = jnp.zeros((depth, 1, LANE), F32)
    gate_bias = gate_bias.at[:, 0, SM_MI:SM_MI + N_HEADS].set(b_ml_i)
    gate_bias = gate_bias.at[:, 0, SM_MF:SM_MF + N_HEADS].set(b_ml_f)
    tri = chunk_tri(t_blk)
    tri_f = tri.astype(F32)
    rep_sel = gate_replicator()
    b_gla_a3 = b_gla_a.reshape(depth, 1, QK_W)
    gla_norm3 = gla_norm.reshape(depth, 1, V_W)
    conv_b3 = conv_b.reshape(depth, 1, 2 * QK_W)
    ml_norm3 = ml_norm.reshape(depth, 1, V_W)
    w_branch2 = w_branch.reshape(depth, N_BRANCH * V_W, d)
    w_gate2 = w_gate.reshape(depth, N_BRANCH * GATE_RANK, d)

    memn = rmsnorm(mem.reshape(MEM_LEN, d), mem_norm, BF16, tm=MEM_LEN)

    for l in range(depth):
        z, wb_b, wg_b, w_out_b = norm_matmul_rows(
            xs, norm_mix[l], w_in_pt, l, tm=TM, tn=TN_IN, out_dtype=F32, transposed_b=True,
            side_cast=(w_branch2, w_gate2, w_out), name="in_proj")
        km = matmul_cols(memn, w_mem_k, l, tm=MEM_LEN, tn=TN_MEM, out_dtype=BF16, name="mem_k")
        vm = matmul_cols(memn, w_mem_v, l, tm=MEM_LEN, tn=TN_MEM, out_dtype=BF16, name="mem_v")
        o_gla = gla_branch(z, tri, tri_f, wa_pad, b_gla_a3, gla_norm3, l, t_blk=t_blk)
        o_ml = mlstm_branch(z, tri, tri_f, rep_sel, conv_w, conv_b3, gate_bias, ml_norm3, l, t_blk=t_blk)
        o_mem = memattn_branch(z, km, vm)
        y, w_ff1_b = merge_branches(o_gla, o_ml, o_mem, z, wb_b, wg_b, b_gate, l,
                                    side_cast=(w_ff1,))
        xs = matmul_acc(y, w_out_b, l, xs, tm=TM, tn=TN_OUT, tk=d, name="out_proj")
        hid, w_ff2_b = norm_matmul_rows(xs, norm_ffn[l], w_ff1_b, l, tm=TM, tn=TN_FFN,
                                        out_dtype=BF16, epilogue="relu2", side_cast=(w_ff2,),
                                        name="ffn_up")
        xs = matmul_acc(hid, w_ff2_b, l, xs, tm=TM, tn=TN_FFN, tk=TK_FFN_DOWN, name="ffn_down")
    return rmsnorm(xs, final_norm, F32).reshape(x.shape)
```
